```python
import math
import jax, jax.numpy as jnp
from jax import lax
import numpy as np

D_MODEL = 1024
BATCH = 16
SEQ = 256
DEPTH = 4
DEC_BATCH = 8
DEC_SEQ = 4096
PAST_LEN = 512

GRID_W = 64
N_EVEN = (DEPTH + 1) // 2
N_ODD = DEPTH // 2
MIX_W = D_MODEL
DH = 64
HA = D_MODEL // 128
NA_KH = 8
NA_KW = 16
DQK = 64
DVB = 2 * DQK
HB = D_MODEL // 256
C_W = D_MODEL // 2
C_GH = 16
C_G = C_W // C_GH
C_P = 64
DT_MIN = 1e-3
DT_MAX = 1e-1
D_W = D_MODEL // 2
HY_ORDER = 2
HY_EMB_BANDS = 16
HY_EMB = 1 + 2 * HY_EMB_BANDS
HY_HID = 64
HY_BAND_MIN = 1e-4
HY_MIN_DECAY = 3.07
HY_MAX_DECAY = 15.35
D_FF = -(-8 * D_MODEL // (3 * 256)) * 256
EVEN_IN = 3 * HA * DH + 2 * HB * 2 * DQK + HB * DVB
ODD_IN = C_W + 3 * D_W
ROPE_BASE = 10000.0
EPS = 1e-6
Q_BLOCK = 128

kernel_name = 'hybrid_diffusion_na_diff_s5_hyena'


def rmsnorm(x, g):
    x32 = x.astype(jnp.float32)
    y = x32 * lax.rsqrt(jnp.mean(x32 * x32, axis=-1, keepdims=True) + EPS)
    return (y * g.astype(jnp.float32)).astype(x.dtype)


def modulation(cond, w, b):
    m = jax.nn.silu(cond) @ w + b
    return [t[:, None, :] for t in jnp.split(m, 6, axis=-1)]


def swiglu(h, wg, wu, wd):
    return (jax.nn.silu(h @ wg) * (h @ wu)) @ wd


def axial_rope(x):
    L, d = x.shape[1], x.shape[-1]
    half = d // 2
    nf = half // 2
    t = jnp.arange(L)
    inv = ROPE_BASE ** (-jnp.arange(nf, dtype=jnp.float32) / nf)
    bshape = (1, L) + (1,) * (x.ndim - 3) + (nf,)

    def rot(xp, pos):
        ang = pos.astype(jnp.float32)[:, None] * inv
        cos = jnp.cos(ang).reshape(bshape)
        sin = jnp.sin(ang).reshape(bshape)
        x1, x2 = xp[..., :nf], xp[..., nf:]
        return jnp.concatenate([x1 * cos - x2 * sin, x1 * sin + x2 * cos], axis=-1)

    out = jnp.concatenate([rot(x[..., :half], t // GRID_W), rot(x[..., half:], t % GRID_W)], axis=-1)
    return out.astype(x.dtype)


def softmax_attn(q, k, v):
    B, Lq, H, Dh = q.shape
    nb = Lq // Q_BLOCK
    qb = jnp.moveaxis(q.reshape(B, nb, Q_BLOCK, H, Dh), 1, 0)
    scale = Dh ** -0.5

    def blk(qi):
        s = jnp.einsum('bqhd,bkhd->bhqk', qi, k, preferred_element_type=jnp.float32) * scale
        p = jax.nn.softmax(s, axis=-1).astype(v.dtype)
        return jnp.einsum('bhqk,bkhd->bqhd', p, v)

    out = lax.map(blk, qb)
    return jnp.moveaxis(out, 0, 1).reshape(B, Lq, H * Dh)


def diff_lambda(lq1, lk1, lq2, lk2, lam_init):
    f32 = jnp.float32
    return (jnp.exp(jnp.sum(lq1.astype(f32) * lk1.astype(f32)))
            - jnp.exp(jnp.sum(lq2.astype(f32) * lk2.astype(f32))) + lam_init)


def diff_attn(q, k, v, lam):
    B, Lq = q.shape[:2]
    nb = Lq // Q_BLOCK
    qb = jnp.moveaxis(q.reshape((B, nb, Q_BLOCK) + q.shape[2:]), 1, 0)
    scale = DQK ** -0.5

    def blk(qi):
        s = jnp.einsum('bqhsd,bkhsd->bhsqk', qi, k, preferred_element_type=jnp.float32) * scale
        p = jax.nn.softmax(s, axis=-1)
        w = (p[:, :, 0] - lam * p[:, :, 1]).astype(v.dtype)
        return jnp.einsum('bhqk,bkhd->bqhd', w, v)

    out = lax.map(blk, qb)
    return jnp.moveaxis(out, 0, 1).reshape(B, Lq, HB, DVB)


def diff_heads_out(o, g, lam_init):
    B, L = o.shape[:2]
    return (rmsnorm(o, g) * (1.0 - lam_init)).reshape(B, L, HB * DVB)


def neighbourhood_attn(q, k, v, kc, vc, rpb):
    B, L, H, Dh = q.shape
    rows = L // GRID_W
    kh, kw = min(NA_KH, rows), NA_KW
    scale = Dh ** -0.5
    qg = q.reshape(B, rows, GRID_W, H, Dh)
    kg = k.reshape(B, rows, GRID_W, H, Dh)
    vg = v.reshape(B, rows, GRID_W, H, Dh)
    cols = np.arange(GRID_W)
    col_idx = np.clip(cols - kw // 2, 0, GRID_W - kw)[:, None] + np.arange(kw)[None, :]
    bias_cols = rpb[:, :, col_idx - cols[:, None] + (NA_KW - 1)]

    def row_block(r):
        rs = jnp.clip(r - kh // 2, 0, rows - kh)
        qr = lax.dynamic_index_in_dim(qg, r, axis=1, keepdims=False)
        kn = lax.dynamic_slice_in_dim(kg, rs, kh, axis=1)[:, :, col_idx]
        vn = lax.dynamic_slice_in_dim(vg, rs, kh, axis=1)[:, :, col_idx]
        drow = rs + jnp.arange(kh) - r + (NA_KH - 1)
        bias = jnp.transpose(bias_cols[:, drow], (0, 2, 1, 3))
        s_loc = jnp.einsum('bqhd,bmqnhd->bhqmn', qr, kn, preferred_element_type=jnp.float32) * scale + bias
        s_ctx = jnp.einsum('bqhd,bkhd->bhqk', qr, kc, preferred_element_type=jnp.float32) * scale
        s = jnp.concatenate([s_loc.reshape(B, H, GRID_W, kh * kw), s_ctx], axis=-1)
        p = jax.nn.softmax(s, axis=-1).astype(v.dtype)
        p_loc = p[..., :kh * kw].reshape(B, H, GRID_W, kh, kw)
        return (jnp.einsum('bhqmn,bmqnhd->bqhd', p_loc, vn)
                + jnp.einsum('bhqk,bkhd->bqhd', p[..., kh * kw:], vc))

    out = lax.map(row_block, jnp.arange(rows))
    return jnp.moveaxis(out, 0, 1).reshape(B, L, H * Dh)


def split_even(p):
    B, L, _ = p.shape
    wa, wq = HA * DH, HB * 2 * DQK
    qa, ka, va, qb, kb, vb = jnp.split(p, [wa, 2 * wa, 3 * wa, 3 * wa + wq, 3 * wa + 2 * wq], axis=-1)
    return (qa.reshape(B, L, HA, DH), ka.reshape(B, L, HA, DH), va.reshape(B, L, HA, DH),
            qb.reshape(B, L, HB, 2, DQK), kb.reshape(B, L, HB, 2, DQK), vb.reshape(B, L, HB, DVB))


def even_mixer_ctx(h, w_in, w_out, lam, lam_init, g_subln):
    B, L, _ = h.shape
    qa, ka, va, qb, kb, vb = split_even(h @ w_in)
    oa = softmax_attn(qa, ka, va)
    ob = diff_heads_out(diff_attn(qb, kb, vb, lam), g_subln, lam_init)
    out = jnp.concatenate([oa, ob], axis=-1) @ w_out
    return out, ka, va, kb.reshape(B, L, HB, 2 * DQK), vb


def even_mixer_lat(h, ctx_ak, ctx_av, ctx_bk, ctx_bv, w_in, w_out, rpb, lam, lam_init, g_subln):
    B, L, _ = h.shape
    qa, ka, va, qb, kb, vb = split_even(h @ w_in)
    oa = neighbourhood_attn(qa, ka, va, ctx_ak, ctx_av, rpb)
    k_all = jnp.concatenate([axial_rope(kb), ctx_bk.reshape(B, ctx_bk.shape[1], HB, 2, DQK)], axis=1)
    v_all = jnp.concatenate([vb, ctx_bv], axis=1)
    ob = diff_heads_out(diff_attn(axial_rope(qb), k_all, v_all, lam), g_subln, lam_init)
    return jnp.concatenate([oa, ob], axis=-1) @ w_out


def _complex_linear_combine(e1, e2):
    a1r, a1i, b1r, b1i = e1
    a2r, a2i, b2r, b2i = e2
    return (a2r * a1r - a2i * a1i, a2r * a1i + a2i * a1r,
            a2r * b1r - a2i * b1i + b2r, a2r * b1i + a2i * b1r + b2i)


def s5_direction(u, lam_re, lam_im, log_dt, b_re, b_im, c_re, c_im, h0, reverse):
    f32 = jnp.float32
    lr, li = lam_re.astype(f32), lam_im.astype(f32)
    dt = jnp.exp(log_dt.astype(f32))[:, None]
    mag = jnp.exp(lr * dt)
    ab_re, ab_im = mag * jnp.cos(li * dt), mag * jnp.sin(li * dt)
    den = lr * lr + li * li
    k_re = ((ab_re - 1.0) * lr + ab_im * li) / den
    k_im = (ab_im * lr - (ab_re - 1.0) * li) / den
    br, bi = b_re.astype(f32), b_im.astype(f32)
    bb_re = k_re[..., None] * br - k_im[..., None] * bi
    bb_im = k_re[..., None] * bi + k_im[..., None] * br
    bu_re = jnp.einsum('gph,blgh->blgp', bb_re, u)
    bu_im = jnp.einsum('gph,blgh->blgp', bb_im, u)
    if h0 is not None:
        h_re, h_im = h0[0].astype(f32), h0[1].astype(f32)
        first = -1 if reverse else 0
        bu_re = bu_re.at[:, first].add(ab_re * h_re - ab_im * h_im)
        bu_im = bu_im.at[:, first].add(ab_re * h_im + ab_im * h_re)
    L = u.shape[1]
    a_re = jnp.broadcast_to(ab_re, (1, L) + ab_re.shape)
    a_im = jnp.broadcast_to(ab_im, (1, L) + ab_im.shape)
    _, _, x_re, x_im = lax.associative_scan(_complex_linear_combine, (a_re, a_im, bu_re, bu_im),
                                            reverse=reverse, axis=1)
    y = (jnp.einsum('ghp,blgp->blgh', c_re.astype(f32), x_re)
         - jnp.einsum('ghp,blgp->blgh', c_im.astype(f32), x_im))
    return y, x_re, x_im


def s5_bidir(u, h0_re, h0_im, lam_re, lam_im, log_dt, b_re, b_im, c_re, c_im, d_skip):
    B, L, _ = u.shape
    u32 = u.astype(jnp.float32)
    ug = u32.reshape(B, L, C_G, C_GH)
    y = d_skip.astype(jnp.float32) * u32
    finals_re, finals_im = [], []
    for dr in range(2):
        h0 = None if h0_re is None else (h0_re[:, dr], h0_im[:, dr])
        yd, x_re, x_im = s5_direction(ug, lam_re[dr], lam_im[dr], log_dt[dr], b_re[dr], b_im[dr],
                                      c_re[dr], c_im[dr], h0, reverse=(dr == 1))
        y = y + yd.reshape(B, L, C_W)
        if h0_re is None:
            last = 0 if dr == 1 else L - 1
            finals_re.append(x_re[:, last])
            finals_im.append(x_im[:, last])
    if h0_re is None:
        return y, jnp.stack(finals_re, axis=1), jnp.stack(finals_im, axis=1)
    return y, None, None


def hyena_filter_spectra(L, w1, b1, fr1, w2, b2, fr2, w3, decay):
    f32 = jnp.float32
    tn = jnp.arange(L, dtype=f32) / L
    bands = jnp.linspace(HY_BAND_MIN, HY_EMB_BANDS - 1, HY_EMB_BANDS, dtype=f32)
    ang = (2.0 * math.pi) * tn[:, None] * bands
    feats = jnp.concatenate([tn[:, None], jnp.cos(ang), jnp.sin(ang)], axis=-1)
    z = jnp.sin(fr1.astype(f32) * (feats @ w1.astype(f32) + b1.astype(f32)))
    z = jnp.sin(fr2.astype(f32) * (z @ w2.astype(f32) + b2.astype(f32)))
    hf = (z @ w3.astype(f32)).reshape(L, 2, HY_ORDER, D_W)
    hf = hf * jnp.exp(-tn[:, None, None, None] * jnp.abs(decay.astype(f32)))
    k = jnp.concatenate([hf[:, 0], jnp.zeros((1, HY_ORDER, D_W), f32), hf[:0:-1, 1]], axis=0)
    return jnp.fft.rfft(k, axis=0)


def hyena(u3, conv_w, conv_b, w1, b1, fr1, w2, b2, fr2, w3, decay, bias):
    B, L, _ = u3.shape
    f32 = jnp.float32
    up = jnp.pad(u3.astype(f32), ((0, 0), (1, 1), (0, 0)))
    cw = conv_w.astype(f32)
    s = up[:, :-2] * cw[0] + up[:, 1:-1] * cw[1] + up[:, 2:] * cw[2] + conv_b.astype(f32)
    v, x1, x2 = jnp.split(s, 3, axis=-1)
    spec = hyena_filter_spectra(L, w1, b1, fr1, w2, b2, fr2, w3, decay)
    bias32 = bias.astype(f32)

    def long_conv(z, n):
        zf = jnp.fft.rfft(z, n=2 * L, axis=1)
        return jnp.fft.irfft(zf * spec[None, :, n], n=2 * L, axis=1)[:, :L]

    z = x1 * (long_conv(v, 0) + bias32[0] * v)
    z = x2 * (long_conv(z, 1) + bias32[1] * z)
    return z.astype(u3.dtype)


def odd_mixer(h, h0_re, h0_im, w_in, w_out, lam_re, lam_im, log_dt, b_re, b_im, c_re, c_im, d_skip,
              w_glu, b_glu, conv_w, conv_b, w1, b1, fr1, w2, b2, fr2, w3, decay, bias):
    p = h @ w_in
    u, hy_in = p[..., :C_W], p[..., C_W:]
    y_s5, fin_re, fin_im = s5_bidir(u, h0_re, h0_im, lam_re, lam_im, log_dt, b_re, b_im, c_re, c_im, d_skip)
    g = jax.nn.gelu(y_s5)
    y_c = (g * jax.nn.sigmoid(g @ w_glu + b_glu)).astype(h.dtype)
    y_d = hyena(hy_in, conv_w, conv_b, w1, b1, fr1, w2, b2, fr2, w3, decay, bias)
    out = jnp.concatenate([y_c, y_d], axis=-1) @ w_out
    return out, fin_re, fin_im


def setup_inputs(seed: int = 0) -> dict:
    key = jax.random.key(seed)
    keys = jax.random.split(key, 80)
    counter = [0]

    def nk():
        counter[0] += 1
        return keys[counter[0] - 1]

    def nrm(shape, s):
        return s * jax.random.normal(nk(), shape, jnp.float32)

    D, F = D_MODEL, D_FF
    inp = {}
    inp['x_prompt'] = nrm((BATCH, SEQ, D), 1.0)
    inp['x_sample'] = nrm((DEC_BATCH, DEC_SEQ, D), 1.0)
    inp['cache_a_k'] = nrm((DEC_BATCH, N_EVEN, PAST_LEN, HA, DH), 1.0)
    inp['cache_a_v'] = nrm((DEC_BATCH, N_EVEN, PAST_LEN, HA, DH), 1.0)
    inp['cache_b_k'] = nrm((DEC_BATCH, N_EVEN, PAST_LEN, HB, 2 * DQK), 1.0)
    inp['cache_b_v'] = nrm((DEC_BATCH, N_EVEN, PAST_LEN, HB, DVB), 1.0)
    inp['state_c_re'] = nrm((DEC_BATCH, N_ODD, 2, C_G, C_P), 0.1)
    inp['state_c_im'] = nrm((DEC_BATCH, N_ODD, 2, C_G, C_P), 0.1)
    inp['c'] = nrm((DEC_BATCH, D), 1.0)
    inp['c_ctx'] = nrm((D,), 1.0)
    inp['w_mod'] = nrm((DEPTH, D, 6 * D), 0.5 * D ** -0.5)
    inp['b_mod'] = nrm((DEPTH, 6 * D), 0.02)
    inp['g_mix_pre'] = 1.0 + nrm((DEPTH, D), 0.02)
    inp['g_mix_post'] = 1.0 + nrm((DEPTH, D), 0.02)
    inp['g_ffn_pre'] = 1.0 + nrm((DEPTH, D), 0.02)
    inp['g_ffn_post'] = 1.0 + nrm((DEPTH, D), 0.02)
    inp['w_ffn_gate'] = nrm((DEPTH, D, F), D ** -0.5)
    inp['w_ffn_up'] = nrm((DEPTH, D, F), D ** -0.5)
    inp['w_ffn_down'] = nrm((DEPTH, F, D), F ** -0.5)
    inp['w_in_e'] = nrm((N_EVEN, D, EVEN_IN), D ** -0.5)
    inp['w_out_e'] = nrm((N_EVEN, MIX_W, D), MIX_W ** -0.5)
    inp['na_rpb'] = nrm((N_EVEN, HA, 2 * NA_KH - 1, 2 * NA_KW - 1), 0.1)
    inp['lam_q1'] = nrm((N_EVEN, DQK), 0.1)
    inp['lam_k1'] = nrm((N_EVEN, DQK), 0.1)
    inp['lam_q2'] = nrm((N_EVEN, DQK), 0.1)
    inp['lam_k2'] = nrm((N_EVEN, DQK), 0.1)
    inp['g_subln'] = 1.0 + nrm((N_EVEN, DVB), 0.02)
    inp['w_in_o'] = nrm((N_ODD, D, ODD_IN), D ** -0.5)
    inp['w_out_o'] = nrm((N_ODD, MIX_W, D), MIX_W ** -0.5)
    inp['ssm_lam_re'] = -0.5 + nrm((N_ODD, 2, C_G, C_P), 0.01)
    inp['ssm_lam_im'] = jnp.pi * jnp.arange(C_P, dtype=jnp.float32) + nrm((N_ODD, 2, C_G, C_P), 0.01)
    inp['ssm_log_dt'] = jax.random.uniform(nk(), (N_ODD, 2, C_G), jnp.float32,
                                           minval=math.log(DT_MIN), maxval=math.log(DT_MAX))
    inp['ssm_b_re'] = nrm((N_ODD, 2, C_G, C_P, C_GH), (2 * C_GH) ** -0.5)
    inp['ssm_b_im'] = nrm((N_ODD, 2, C_G, C_P, C_GH), (2 * C_GH) ** -0.5)
    inp['ssm_c_re'] = nrm((N_ODD, 2, C_G, C_GH, C_P), 0.5)
    inp['ssm_c_im'] = nrm((N_ODD, 2, C_G, C_GH, C_P), 0.5)
    inp['ssm_d'] = nrm((N_ODD, C_W), 1.0)
    inp['w_glu'] = nrm((N_ODD, C_W, C_W), C_W ** -0.5)
    inp['b_glu'] = nrm((N_ODD, C_W), 0.02)
    inp['hy_conv_w'] = nrm((N_ODD, 3, 3 * D_W), 3 ** -0.5)
    inp['hy_conv_b'] = nrm((N_ODD, 3 * D_W), 0.02)
    inp['hy_w1'] = nrm((N_ODD, HY_EMB, HY_HID), HY_EMB ** -0.5)
    inp['hy_b1'] = nrm((N_ODD, HY_HID), 0.1)
    inp['hy_fr1'] = 1.0 + nrm((N_ODD, HY_HID), 0.02)
    inp['hy_w2'] = nrm((N_ODD, HY_HID, HY_HID), HY_HID ** -0.5)
    inp['hy_b2'] = nrm((N_ODD, HY_HID), 0.1)
    inp['hy_fr2'] = 1.0 + nrm((N_ODD, HY_HID), 0.02)
    inp['hy_w3'] = nrm((N_ODD, HY_HID, 2 * HY_ORDER * D_W), 0.1 * HY_HID ** -0.5)
    inp['hy_decay'] = (jnp.linspace(HY_MIN_DECAY, HY_MAX_DECAY, D_W, dtype=jnp.float32)
                       + nrm((N_ODD, 2, HY_ORDER, D_W), 0.1))
    inp['hy_bias'] = nrm((N_ODD, HY_ORDER, D_W), 1.0)
    return inp


def reference(x_prompt, x_sample, cache_a_k, cache_a_v, cache_b_k, cache_b_v, state_c_re, state_c_im,
              c, c_ctx, w_mod, b_mod, g_mix_pre, g_mix_post, g_ffn_pre, g_ffn_post,
              w_ffn_gate, w_ffn_up, w_ffn_down, w_in_e, w_out_e, na_rpb,
              lam_q1, lam_k1, lam_q2, lam_k2, g_subln, w_in_o, w_out_o,
              ssm_lam_re, ssm_lam_im, ssm_log_dt, ssm_b_re, ssm_b_im, ssm_c_re, ssm_c_im, ssm_d,
              w_glu, b_glu, hy_conv_w, hy_conv_b, hy_w1, hy_b1, hy_fr1, hy_w2, hy_b2, hy_fr2,
              hy_w3, hy_decay, hy_bias):
    xc, xs = x_prompt, x_sample
    new_ak, new_av, new_bk, new_bv, new_sre, new_sim = [], [], [], [], [], []
    for l in range(DEPTH):
        mod_c = modulation(c_ctx[None, :], w_mod[l], b_mod[l])
        mod_s = modulation(c, w_mod[l], b_mod[l])
        hc = rmsnorm(xc, g_mix_pre[l]) * (1.0 + mod_c[1]) + mod_c[0]
        hs = rmsnorm(xs, g_mix_pre[l]) * (1.0 + mod_s[1]) + mod_s[0]
        if l % 2 == 0:
            e = l // 2
            lam_init = 0.8 - 0.6 * math.exp(-0.3 * l)
            lam = diff_lambda(lam_q1[e], lam_k1[e], lam_q2[e], lam_k2[e], lam_init)
            oc, ak, av, bk, bv = even_mixer_ctx(hc, w_in_e[e], w_out_e[e], lam, lam_init, g_subln[e])
            os_ = even_mixer_lat(hs, cache_a_k[:, e], cache_a_v[:, e], cache_b_k[:, e], cache_b_v[:, e],
                                 w_in_e[e], w_out_e[e], na_rpb[e], lam, lam_init, g_subln[e])
            new_ak.append(ak)
            new_av.append(av)
            new_bk.append(bk)
            new_bv.append(bv)
        else:
            o = l // 2
            odd_p = (w_in_o[o], w_out_o[o], ssm_lam_re[o], ssm_lam_im[o], ssm_log_dt[o], ssm_b_re[o],
                     ssm_b_im[o], ssm_c_re[o], ssm_c_im[o], ssm_d[o], w_glu[o], b_glu[o],
                     hy_conv_w[o], hy_conv_b[o], hy_w1[o], hy_b1[o], hy_fr1[o], hy_w2[o], hy_b2[o],
                     hy_fr2[o], hy_w3[o], hy_decay[o], hy_bias[o])
            oc, sre, sim = odd_mixer(hc, None, None, *odd_p)
            os_, _, _ = odd_mixer(hs, state_c_re[:, o], state_c_im[:, o], *odd_p)
            new_sre.append(sre)
            new_sim.append(sim)
        xc = xc + mod_c[2] * rmsnorm(oc, g_mix_post[l])
        xs = xs + mod_s[2] * rmsnorm(os_, g_mix_post[l])
        hc = rmsnorm(xc, g_ffn_pre[l]) * (1.0 + mod_c[4]) + mod_c[3]
        hs = rmsnorm(xs, g_ffn_pre[l]) * (1.0 + mod_s[4]) + mod_s[3]
        xc = xc + mod_c[5] * rmsnorm(swiglu(hc, w_ffn_gate[l], w_ffn_up[l], w_ffn_down[l]), g_ffn_post[l])
        xs = xs + mod_s[5] * rmsnorm(swiglu(hs, w_ffn_gate[l], w_ffn_up[l], w_ffn_down[l]), g_ffn_post[l])
    new_a_k = jnp.stack(new_ak, axis=1)
    new_a_v = jnp.stack(new_av, axis=1)
    new_b_k = jnp.stack(new_bk, axis=1)
    new_b_v = jnp.stack(new_bv, axis=1)
    new_s_re = jnp.stack(new_sre, axis=1)
    new_s_im = jnp.stack(new_sim, axis=1)
    return (xc, xs, new_a_k, new_a_v, new_b_k, new_b_v, new_s_re, new_s_im)
```

```python
import functools
import math

import numpy as np
import jax
import jax.numpy as jnp
from jax import lax
from jax.experimental import pallas as pl
from jax.experimental.pallas import tpu as pltpu

f32 = jnp.float32
bf16 = jnp.bfloat16

D = 1024
DEPTH = 4
D_FF = 2816
GRID_W = 64
HA, DH = 8, 64
HB, DQK, DVB = 4, 64, 128
NA_KH, NA_KW = 8, 16
C_W, C_G, C_GH, C_P = 512, 32, 16, 64
D_W = 512
HY_EMB_BANDS = 16
HY_BAND_MIN = 1e-4
ROPE_BASE = 10000.0
EPS = 1e-6
DT_MIN = 1e-3
DT_MAX = 1e-1
NEG = -1e30

LANES = 128
VMEM_LIMIT = 56 * 1024 * 1024


def _cp(*sem):
    return pltpu.CompilerParams(dimension_semantics=sem, vmem_limit_bytes=VMEM_LIMIT)


def _bdot(a, b):
    return jnp.dot(a.astype(bf16), b.astype(bf16), preferred_element_type=f32)


def _bdot_t(a, b):
    return lax.dot_general(a.astype(bf16), b.astype(bf16), (((1,), (1,)), ((), ())),
                           preferred_element_type=f32)


def _rms(x, g):
    return x * lax.rsqrt(jnp.mean(x * x, axis=-1, keepdims=True) + EPS) * g


def _silu(x):
    return x * (1.0 / (1.0 + jnp.exp(-x)))


def _resident(shape):
    nd = len(shape)
    return pl.BlockSpec(shape, lambda *_: (0,) * nd, pipeline_mode=pl.Buffered(1))


def _mod_kernel(c_ref, w_ref, b_ref, o_ref):
    o_ref[0] = _bdot(_silu(c_ref[...]), w_ref[0]) + b_ref[0]


def modulation(cond, w_mod, b_mod):
    nb = 4
    cb = 6 * D // nb
    return pl.pallas_call(
        _mod_kernel,
        out_shape=jax.ShapeDtypeStruct((DEPTH, 16, 6 * D), f32),
        grid=(DEPTH, nb),
        in_specs=[pl.BlockSpec((16, D), lambda l, j: (0, 0)),
                  pl.BlockSpec((1, D, cb), lambda l, j: (l, 0, j)),
                  pl.BlockSpec((1, 1, cb), lambda l, j: (l, 0, j))],
        out_specs=pl.BlockSpec((1, 16, cb), lambda l, j: (l, 0, j)),
        compiler_params=_cp("arbitrary", "arbitrary"),
        name="modulation",
    )(cond, w_mod, b_mod.reshape(DEPTH, 1, 6 * D))


def _mod_slice(m_ref, k):
    return m_ref[0, :, k * D:(k + 1) * D]


def _mod_map(n_seq, tiles_per_seq):
    if n_seq == 1:
        return lambda i: (0, 0, 0)
    return lambda i: (i // tiles_per_seq, 0, 0)


def _even_proj_kernel(x_ref, m_ref, g_ref, w_ref, cs_ref, sn_ref, o_ref, *, rope):
    h = _rms(x_ref[...], g_ref[...]) * (1.0 + _mod_slice(m_ref, 1)) + _mod_slice(m_ref, 0)
    p = _bdot(h, w_ref[...])
    if rope:
        lane = lax.broadcasted_iota(jnp.int32, (1, LANES), 1)
        first = (lane % 32) < 16
        c, s = cs_ref[...], sn_ref[...]
    for j in range(24):
        blk = p[:, j * LANES:(j + 1) * LANES]
        if rope and 12 <= j < 20:
            partner = jnp.where(first, pltpu.roll(blk, LANES - 16, axis=1), pltpu.roll(blk, 16, axis=1))
            blk = blk * c + partner * s
        o_ref[j] = blk.astype(o_ref.dtype)


def even_proj(x, mod, g, w, cos_t, sin_t, *, seq_len, rope, out_dtype, tm):
    rows = x.shape[0]
    n_seq = mod.shape[0]
    tps = seq_len // tm if n_seq > 1 else 1
    tbl_map = (lambda i: (i % tps, 0)) if rope else (lambda i: (0, 0))
    return pl.pallas_call(
        functools.partial(_even_proj_kernel, rope=rope),
        out_shape=jax.ShapeDtypeStruct((24, rows, LANES), out_dtype),
        grid=(rows // tm,),
        in_specs=[pl.BlockSpec((tm, D), lambda i: (i, 0)),
                  pl.BlockSpec((1, 1, 6 * D), _mod_map(n_seq, tps)),
                  _resident((1, D)),
                  _resident((D, 3 * D)),
                  pl.BlockSpec((tm, LANES), tbl_map),
                  pl.BlockSpec((tm, LANES), tbl_map)],
        out_specs=pl.BlockSpec((24, tm, LANES), lambda i: (0, i, 0)),
        compiler_params=_cp("arbitrary"),
        name="even_proj",
    )(x, mod, g, w, cos_t, sin_t)


def _even_out_kernel(x_ref, oa_ref, ob_ref, m_ref, g_ref, w_ref, o_ref):
    w = w_ref[...]
    o = _bdot(oa_ref[...], w[:D // 2]) + _bdot(ob_ref[...], w[D // 2:])
    o_ref[...] = x_ref[...] + _mod_slice(m_ref, 2) * _rms(o, g_ref[...])


def even_out(x, oa, ob, mod, g, w, *, seq_len, tm):
    rows = x.shape[0]
    n_seq = mod.shape[0]
    tps = seq_len // tm if n_seq > 1 else 1
    return pl.pallas_call(
        _even_out_kernel,
        out_shape=jax.ShapeDtypeStruct((rows, D), f32),
        grid=(rows // tm,),
        in_specs=[pl.BlockSpec((tm, D), lambda i: (i, 0)),
                  pl.BlockSpec((tm, D // 2), lambda i: (i, 0)),
                  pl.BlockSpec((tm, D // 2), lambda i: (i, 0)),
                  pl.BlockSpec((1, 1, 6 * D), _mod_map(n_seq, tps)),
                  _resident((1, D)),
                  _resident((D, D))],
        out_specs=pl.BlockSpec((tm, D), lambda i: (i, 0)),
        compiler_params=_cp("arbitrary"),
        name="even_out",
    )(x, oa, ob, mod, g, w)


FF_CHUNK = D_FF // 2


def _ffn_kernel(x_ref, m_ref, gpre_ref, gpost_ref, wg_ref, wu_ref, wd_ref, o_ref):
    x = x_ref[...]
    h = (_rms(x, gpre_ref[...]) * (1.0 + _mod_slice(m_ref, 4)) + _mod_slice(m_ref, 3)).astype(bf16)
    acc = None
    for c in range(D_FF // FF_CHUNK):
        sl = slice(c * FF_CHUNK, (c + 1) * FF_CHUNK)
        a = _silu(_bdot(h, wg_ref[:, sl])) * _bdot(h, wu_ref[:, sl])
        part = _bdot(a, wd_ref[sl, :])
        acc = part if acc is None else acc + part
    o_ref[...] = x + _mod_slice(m_ref, 5) * _rms(acc, gpost_ref[...])


def ffn(x, mod, gpre, gpost, wg, wu, wd, *, seq_len, tm):
    rows = x.shape[0]
    n_seq = mod.shape[0]
    tps = seq_len // tm if n_seq > 1 else 1
    return pl.pallas_call(
        _ffn_kernel,
        out_shape=jax.ShapeDtypeStruct((rows, D), f32),
        grid=(rows // tm,),
        in_specs=[pl.BlockSpec((tm, D), lambda i: (i, 0)),
                  pl.BlockSpec((1, 1, 6 * D), _mod_map(n_seq, tps)),
                  _resident((1, D)),
                  _resident((1, D)),
                  _resident((D, D_FF)),
                  _resident((D, D_FF)),
                  _resident((D_FF, D))],
        out_specs=pl.BlockSpec((tm, D), lambda i: (i, 0)),
        compiler_params=_cp("arbitrary"),
        name="ffn",
    )(x, mod, gpre, gpost, wg, wu, wd)


def rope_tables(seq_len):
    lane = np.arange(LANES)
    d = lane % DQK
    nf = DQK // 4
    inv = ROPE_BASE ** (-jnp.arange(nf, dtype=f32) / nf)
    t = jnp.arange(seq_len)
    pos = jnp.where((d < DQK // 2)[None, :], (t // GRID_W)[:, None], (t % GRID_W)[:, None]).astype(f32)
    ang = pos * inv[d % nf][None, :]
    sign = np.where((d % (DQK // 2)) < nf, -1.0, 1.0).astype(np.float32)
    return jnp.cos(ang), jnp.sin(ang) * sign[None, :]


def _diff_lambda(lam_ref, lam_init):
    l = lam_ref[...]
    a1 = jnp.sum(l[0:1] * l[1:2], axis=1, keepdims=True)
    a2 = jnp.sum(l[2:3] * l[3:4], axis=1, keepdims=True)
    return jnp.exp(a1) - jnp.exp(a2) + lam_init


def _half_masks():
    lane = lax.broadcasted_iota(jnp.int32, (1, LANES), 1)
    lo = lane < (LANES // 2)
    return lo, jnp.logical_not(lo)


def _softmax_parts(s):
    m = jnp.max(s, axis=-1, keepdims=True)
    e = jnp.exp(s - m)
    return e, jnp.sum(e, axis=-1, keepdims=True)


def _ctx_attn_kernel(p_ref, lam_ref, gs_ref, oa_ref, ob_ref, *, lam_init):
    lo, hi = _half_masks()
    scale = DH ** -0.5
    lam = _diff_lambda(lam_ref, lam_init)
    for pr in range(HA // 2):
        q, k, v = p_ref[pr], p_ref[4 + pr], p_ref[8 + pr]
        outs = []
        for msk in (lo, hi):
            e, l = _softmax_parts(_bdot_t(jnp.where(msk, q, 0.0), k) * scale)
            outs.append(_bdot(e, v) / l)
        oa_ref[:, pr * LANES:(pr + 1) * LANES] = jnp.where(lo, outs[0], outs[1])
    for h in range(HB):
        q, k, v = p_ref[12 + h], p_ref[16 + h], p_ref[20 + h]
        ps = []
        for msk in (lo, hi):
            e, l = _softmax_parts(_bdot_t(jnp.where(msk, q, 0.0), k) * scale)
            ps.append(e / l)
        o = _bdot(ps[0] - lam * ps[1], v)
        ob_ref[:, h * LANES:(h + 1) * LANES] = _rms(o, gs_ref[...]) * (1.0 - lam_init)


def ctx_even_attn(p, lamv, gs, *, lam_init, n_seq, seq_len):
    rows = n_seq * seq_len
    return pl.pallas_call(
        functools.partial(_ctx_attn_kernel, lam_init=lam_init),
        out_shape=(jax.ShapeDtypeStruct((rows, D // 2), f32), jax.ShapeDtypeStruct((rows, D // 2), f32)),
        grid=(n_seq,),
        in_specs=[pl.BlockSpec((24, seq_len, LANES), lambda b: (0, b, 0)),
                  _resident((8, LANES)),
                  _resident((1, DVB))],
        out_specs=(pl.BlockSpec((seq_len, D // 2), lambda b: (b, 0)),
                   pl.BlockSpec((seq_len, D // 2), lambda b: (b, 0))),
        compiler_params=_cp("arbitrary"),
        name="ctx_even_attn",
    )(p, lamv, gs)


NA_QROWS = 4
NA_KROWS = NA_QROWS + NA_KH
NA_TBL = 2 * NA_KH


def na_bias_tables(rpb):
    cols = np.arange(GRID_W)
    cs = np.clip(cols - NA_KW // 2, 0, GRID_W - NA_KW)
    cp = cols[None, :]
    inwin = (cp >= cs[:, None]) & (cp < cs[:, None] + NA_KW)
    dcol = np.clip(cp - cols[:, None] + (NA_KW - 1), 0, 2 * NA_KW - 2)
    tm = jnp.where(inwin[None, None], rpb[:, :, dcol], NEG)
    tm = jnp.concatenate([tm, jnp.full((HA, 1, GRID_W, GRID_W), NEG, f32)], axis=1)
    z = jnp.zeros_like(tm)
    return jnp.concatenate([tm, z], axis=-1), jnp.concatenate([z, tm], axis=-1)


def _na_kernel(q_ref, k_ref, v_ref, kc_ref, vc_ref, tl_ref, tr_ref, o_ref, *, n_rows):
    i = pl.program_id(2)
    ws = jnp.clip(NA_QROWS * i - NA_KH // 2, 0, n_rows - NA_KROWS)
    start = pl.multiple_of(ws * GRID_W, GRID_W)
    q = q_ref[0]
    kl = k_ref[0, pl.ds(start, NA_KROWS * GRID_W), :]
    vl = v_ref[0, pl.ds(start, NA_KROWS * GRID_W), :]
    kc = kc_ref[0, 0].astype(bf16)
    vc = vc_ref[0, 0].astype(bf16)
    lo, hi = _half_masks()
    scale = DH ** -0.5
    outs = []
    for hh, msk in enumerate((lo, hi)):
        qm = jnp.where(msk, q, jnp.zeros_like(q))
        s_loc = _bdot_t(qm, kl) * scale
        s_ctx = _bdot_t(qm, kc) * scale
        bias_rows = []
        for qr in range(NA_QROWS):
            r = NA_QROWS * i + qr
            rs = jnp.clip(r - NA_KH // 2, 0, n_rows - NA_KH)

            def tbl_idx(kr):
                rp = ws + kr
                valid = jnp.logical_and(rp >= rs, rp < rs + NA_KH)
                return jnp.where(valid, rp - r + (NA_KH - 1), NA_TBL - 1)

            tiles = [tl_ref[hh, tbl_idx(2 * kp)] + tr_ref[hh, tbl_idx(2 * kp + 1)]
                     for kp in range(NA_KROWS // 2)]
            bias_rows.append(jnp.concatenate(tiles, axis=1))
        s_loc = s_loc + jnp.concatenate(bias_rows, axis=0)
        m = jnp.maximum(jnp.max(s_loc, axis=-1, keepdims=True), jnp.max(s_ctx, axis=-1, keepdims=True))
        e_loc = jnp.exp(s_loc - m)
        e_ctx = jnp.exp(s_ctx - m)
        l = jnp.sum(e_loc, axis=-1, keepdims=True) + jnp.sum(e_ctx, axis=-1, keepdims=True)
        outs.append((_bdot(e_loc, vl) + _bdot(e_ctx, vc)) / l)
    o_ref[...] = jnp.where(lo, outs[0], outs[1])


def lat_na_attn(p, cache_k, cache_v, tl, tr, *, e, n_seq, seq_len):
    n_rows = seq_len // GRID_W
    steps = n_rows // NA_QROWS
    tq = NA_QROWS * GRID_W
    past = cache_k.shape[2]
    return pl.pallas_call(
        functools.partial(_na_kernel, n_rows=n_rows),
        out_shape=jax.ShapeDtypeStruct((n_seq * seq_len, D // 2), f32),
        grid=(n_seq, HA // 2, steps),
        in_specs=[pl.BlockSpec((1, tq, LANES), lambda b, pr, i: (pr, b * steps + i, 0)),
                  pl.BlockSpec((1, seq_len, LANES), lambda b, pr, i: (4 + pr, b, 0)),
                  pl.BlockSpec((1, seq_len, LANES), lambda b, pr, i: (8 + pr, b, 0)),
                  pl.BlockSpec((1, 1, past, LANES), lambda b, pr, i: (b, e, 0, pr)),
                  pl.BlockSpec((1, 1, past, LANES), lambda b, pr, i: (b, e, 0, pr)),
                  pl.BlockSpec((2, NA_TBL, GRID_W, LANES), lambda b, pr, i: (pr, 0, 0, 0)),
                  pl.BlockSpec((2, NA_TBL, GRID_W, LANES), lambda b, pr, i: (pr, 0, 0, 0))],
        out_specs=pl.BlockSpec((tq, LANES), lambda b, pr, i: (b * steps + i, pr)),
        compiler_params=_cp("arbitrary", "arbitrary", "arbitrary"),
        name="lat_na_attn",
    )(p, p, p, cache_k, cache_v, tl, tr)


DIFF_TQ = 512
DIFF_TK = 512


def _diff_kernel(q_ref, k_ref, v_ref, kc_ref, vc_ref, lam_ref, gs_ref, o_ref, *, lam_init, seq_len):
    lo, hi = _half_masks()
    scale = DQK ** -0.5
    q = q_ref[0]
    qms = [jnp.where(msk, q, jnp.zeros_like(q)) for msk in (lo, hi)]
    tq = q.shape[0]

    def absorb(kb, vb, carry):
        new = []
        for qm, (m, l, a) in zip(qms, carry):
            s = _bdot_t(qm, kb) * scale
            mn = jnp.maximum(m, jnp.max(s, axis=-1, keepdims=True))
            alpha = jnp.exp(m - mn)
            e = jnp.exp(s - mn)
            new.append((mn, alpha * l + jnp.sum(e, axis=-1, keepdims=True), alpha * a + _bdot(e, vb)))
        return tuple(new)

    def body(j, carry):
        off = pl.multiple_of(j * DIFF_TK, DIFF_TK)
        return absorb(k_ref[0, pl.ds(off, DIFF_TK), :], v_ref[0, pl.ds(off, DIFF_TK), :], carry)

    one = (jnp.full((tq, 1), NEG, f32), jnp.zeros((tq, 1), f32), jnp.zeros((tq, LANES), f32))
    carry = lax.fori_loop(0, seq_len // DIFF_TK, body, (one, one))
    (_, l0, a0), (_, l1, a1) = absorb(kc_ref[0, 0].astype(bf16), vc_ref[0, 0].astype(bf16), carry)
    o = a0 / l0 - _diff_lambda(lam_ref, lam_init) * (a1 / l1)
    o_ref[...] = _rms(o, gs_ref[...]) * (1.0 - lam_init)


def lat_diff_attn(p, cache_k, cache_v, lamv, gs, *, e, lam_init, n_seq, seq_len):
    steps = seq_len // DIFF_TQ
    past = cache_k.shape[2]
    return pl.pallas_call(
        functools.partial(_diff_kernel, lam_init=lam_init, seq_len=seq_len),
        out_shape=jax.ShapeDtypeStruct((n_seq * seq_len, D // 2), f32),
        grid=(n_seq, HB, steps),
        in_specs=[pl.BlockSpec((1, DIFF_TQ, LANES), lambda b, h, i: (12 + h, b * steps + i, 0)),
                  pl.BlockSpec((1, seq_len, LANES), lambda b, h, i: (16 + h, b, 0)),
                  pl.BlockSpec((1, seq_len, LANES), lambda b, h, i: (20 + h, b, 0)),
                  pl.BlockSpec((1, 1, past, LANES), lambda b, h, i: (b, e, 0, h)),
                  pl.BlockSpec((1, 1, past, LANES), lambda b, h, i: (b, e, 0, h)),
                  _resident((8, LANES)),
                  _resident((1, DVB))],
        out_specs=pl.BlockSpec((DIFF_TQ, LANES), lambda b, h, i: (b * steps + i, h)),
        compiler_params=_cp("arbitrary", "arbitrary", "arbitrary"),
        name="lat_diff_attn",
    )(p, p, p, cache_k, cache_v, lamv, gs)


def _odd_proj_kernel(x_ref, m_ref, g_ref, w_ref, u_ref, hy_ref, *, nb):
    g = g_ref[...]
    w = w_ref[...]
    for b in range(nb):
        m = m_ref[b]
        h = _rms(x_ref[b], g) * (1.0 + m[:, D:2 * D]) + m[:, :D]
        p = _bdot(h, w)
        u_ref[:, b, :] = p[:, :C_W]
        hy_ref[b] = p[:, C_W:]


def odd_proj(x, mod, g, w, *, tt):
    nb, seq_len, _ = x.shape
    return pl.pallas_call(
        functools.partial(_odd_proj_kernel, nb=nb),
        out_shape=(jax.ShapeDtypeStruct((seq_len, nb, C_W), f32),
                   jax.ShapeDtypeStruct((nb, seq_len, 3 * D_W), f32)),
        grid=(seq_len // tt,),
        in_specs=[pl.BlockSpec((nb, tt, D), lambda i: (0, i, 0)),
                  _resident((nb, 1, 6 * D)),
                  _resident((1, D)),
                  _resident((D, C_W + 3 * D_W))],
        out_specs=(pl.BlockSpec((tt, nb, C_W), lambda i: (i, 0, 0)),
                   pl.BlockSpec((nb, tt, 3 * D_W), lambda i: (0, i, 0))),
        compiler_params=_cp("arbitrary"),
        name="odd_proj",
    )(x, mod, g, w)


def _odd_out_kernel(x_ref, yc_ref, yd_ref, m_ref, g_ref, w_ref, o_ref, *, nb):
    g = g_ref[...]
    w = w_ref[...]
    for b in range(nb):
        o = _bdot(yc_ref[:, b, :], w[:C_W]) + _bdot(yd_ref[b], w[C_W:])
        o_ref[b] = x_ref[b] + m_ref[b][:, 2 * D:3 * D] * _rms(o, g)


def odd_out(x, yc, yd, mod, g, w, *, tt):
    nb, seq_len, _ = x.shape
    return pl.pallas_call(
        functools.partial(_odd_out_kernel, nb=nb),
        out_shape=jax.ShapeDtypeStruct((nb, seq_len, D), f32),
        grid=(seq_len // tt,),
        in_specs=[pl.BlockSpec((nb, tt, D), lambda i: (0, i, 0)),
                  pl.BlockSpec((tt, nb, C_W), lambda i: (i, 0, 0)),
                  pl.BlockSpec((nb, tt, D_W), lambda i: (0, i, 0)),
                  _resident((nb, 1, 6 * D)),
                  _resident((1, D)),
                  _resident((D, D))],
        out_specs=pl.BlockSpec((nb, tt, D), lambda i: (0, i, 0)),
        compiler_params=_cp("arbitrary"),
        name="odd_out",
    )(x, yc, yd, mod, g, w)


S5_STATES = C_G * C_P
S5_SB = 512


def s5_prepare(lam_re, lam_im, log_dt, b_re, b_im, c_re, c_im):
    lr, li = lam_re.astype(f32), lam_im.astype(f32)
    dt = jnp.exp(log_dt.astype(f32))[:, None]
    mag = jnp.exp(lr * dt)
    ab_re, ab_im = mag * jnp.cos(li * dt), mag * jnp.sin(li * dt)
    den = lr * lr + li * li
    k_re = ((ab_re - 1.0) * lr + ab_im * li) / den
    k_im = (ab_im * lr - (ab_re - 1.0) * li) / den
    br, bi = b_re.astype(f32), b_im.astype(f32)
    bb_re = k_re[..., None] * br - k_im[..., None] * bi
    bb_im = k_re[..., None] * bi + k_im[..., None] * br
    eye = jnp.eye(C_G // 2, dtype=f32)

    def in_blocks(bb):
        return jnp.einsum('agph,gk->aghkp', bb.reshape(2, C_G // 2, C_P, C_GH), eye).reshape(2, C_W // 2, S5_STATES // 2)

    def out_blocks(c):
        return jnp.einsum('aghp,gk->agpkh', c.reshape(2, C_G // 2, C_GH, C_P), eye).reshape(2, S5_STATES // 2, C_W // 2)

    bm = jnp.concatenate([in_blocks(bb_re), in_blocks(bb_im)], axis=-1).astype(bf16)
    cm = jnp.concatenate([out_blocks(c_re.astype(f32)), -out_blocks(c_im.astype(f32))], axis=1).astype(bf16)
    a = jnp.concatenate([ab_re.reshape(1, S5_STATES), ab_im.reshape(1, S5_STATES)], axis=1)
    return a, bm, cm


def _gelu_tanh(x):
    return 0.5 * x * (1.0 + jnp.tanh(math.sqrt(2.0 / math.pi) * (x + 0.044715 * (x * x * x))))


def _s5_kernel(u_ref, add_ref, h0_ref, a_ref, bm_ref, cm_ref, wg_ref, bg_ref, y_ref, fin_ref, xs_ref, car_ref,
               *, nb, tc, reverse):
    i = pl.program_id(0)
    half_s = S5_STATES // 2

    @pl.when(i == 0)
    def _():
        car_ref[...] = h0_ref[...]

    u = u_ref[...]
    for hf in range(2):
        r = _bdot(u[:, hf * (C_W // 2):(hf + 1) * (C_W // 2)], bm_ref[hf])
        xs_ref[:, hf * half_s:(hf + 1) * half_s] = r[:, :half_s]
        xs_ref[:, S5_STATES + hf * half_s:S5_STATES + (hf + 1) * half_s] = r[:, half_s:]

    for sb in range(S5_STATES // S5_SB):
        cre = slice(sb * S5_SB, (sb + 1) * S5_SB)
        cim = slice(S5_STATES + sb * S5_SB, S5_STATES + (sb + 1) * S5_SB)
        ar = jnp.broadcast_to(a_ref[:, cre], (nb, S5_SB))
        ai = jnp.broadcast_to(a_ref[:, cim], (nb, S5_SB))

        def body(s, carry):
            xr, xi = carry
            t = (tc - 1 - s) if reverse else s
            r0 = pl.multiple_of(t * nb, nb)
            nr = ar * xr - ai * xi + xs_ref[pl.ds(r0, nb), cre]
            ni = ar * xi + ai * xr + xs_ref[pl.ds(r0, nb), cim]
            xs_ref[pl.ds(r0, nb), cre] = nr
            xs_ref[pl.ds(r0, nb), cim] = ni
            return nr, ni

        xr, xi = lax.fori_loop(0, tc, body, (car_ref[:, cre], car_ref[:, cim]), unroll=4)
        car_ref[:, cre] = xr
        car_ref[:, cim] = xi

    ys = []
    for hf in range(2):
        xre = xs_ref[:, hf * half_s:(hf + 1) * half_s]
        xim = xs_ref[:, S5_STATES + hf * half_s:S5_STATES + (hf + 1) * half_s]
        ys.append(_bdot(xre, cm_ref[hf, :half_s]) + _bdot(xim, cm_ref[hf, half_s:]))
    y = jnp.concatenate(ys, axis=1)
    if reverse:
        y = y + add_ref[...]
        g = _gelu_tanh(y)
        y = g * (1.0 / (1.0 + jnp.exp(-(_bdot(g, wg_ref[...]) + bg_ref[...]))))
    else:
        y = y + add_ref[...] * u
    y_ref[...] = y
    fin_ref[...] = car_ref[...]


def s5_direction(u, add, h0, a, bm, cm, w_glu, b_glu, *, nb, tc, reverse):
    rows = u.shape[0]
    rb = tc * nb
    nc = rows // rb
    cmap = (lambda i: (nc - 1 - i, 0)) if reverse else (lambda i: (i, 0))
    add_spec = pl.BlockSpec((rb, C_W), cmap) if reverse else _resident((1, C_W))
    return pl.pallas_call(
        functools.partial(_s5_kernel, nb=nb, tc=tc, reverse=reverse),
        out_shape=(jax.ShapeDtypeStruct((rows, C_W), f32), jax.ShapeDtypeStruct((nb, 2 * S5_STATES), f32)),
        grid=(nc,),
        in_specs=[pl.BlockSpec((rb, C_W), cmap),
                  add_spec,
                  _resident((nb, 2 * S5_STATES)),
                  _resident((1, 2 * S5_STATES)),
                  _resident((2, C_W // 2, S5_STATES)),
                  _resident((2, S5_STATES, C_W // 2)),
                  _resident((C_W, C_W)),
                  _resident((1, C_W))],
        out_specs=(pl.BlockSpec((rb, C_W), cmap),
                   pl.BlockSpec((nb, 2 * S5_STATES), lambda i: (0, 0))),
        scratch_shapes=[pltpu.VMEM((rb, 2 * S5_STATES), f32), pltpu.VMEM((nb, 2 * S5_STATES), f32)],
        compiler_params=_cp("arbitrary"),
        name="s5_rev" if reverse else "s5_fwd",
    )(u, add, h0, a, bm, cm, w_glu, b_glu)


def _sconv_kernel(x_ref, w_ref, b_ref, o_ref):
    x = x_ref[0]
    n = x.shape[0]
    row = lax.broadcasted_iota(jnp.int32, (n, 1), 0)
    prev = jnp.where(row == 0, 0.0, pltpu.roll(x, 1, axis=0))
    nxt = jnp.where(row == n - 1, 0.0, pltpu.roll(x, n - 1, axis=0))
    w = w_ref[...]
    o_ref[0] = prev * w[0:1] + x * w[1:2] + nxt * w[2:3] + b_ref[...]


def short_conv(x, w, b):
    nb, seq_len, ch = x.shape
    return pl.pallas_call(
        _sconv_kernel,
        out_shape=jax.ShapeDtypeStruct(x.shape, f32),
        grid=(nb, ch // LANES),
        in_specs=[pl.BlockSpec((1, seq_len, LANES), lambda b_, j: (b_, 0, j)),
                  pl.BlockSpec((3, LANES), lambda b_, j: (0, j)),
                  pl.BlockSpec((1, LANES), lambda b_, j: (0, j))],
        out_specs=pl.BlockSpec((1, seq_len, LANES), lambda b_, j: (b_, 0, j)),
        compiler_params=_cp("arbitrary", "arbitrary"),
        name="short_conv",
    )(x, w, b)


HY_PAD = LANES
HY_OUT = 4 * D_W


def _filter_kernel(f_ref, tn_ref, w1_ref, b1_ref, fr1_ref, w2_ref, b2_ref, fr2_ref, w3_ref, dec_ref, o_ref):
    z = jnp.sin(fr1_ref[...] * (_bdot(f_ref[...], w1_ref[...]) + b1_ref[...]))
    z = jnp.sin(fr2_ref[...] * (_bdot(z, w2_ref[...]) + b2_ref[...]))
    o_ref[...] = _bdot(z, w3_ref[...]) * jnp.exp(-tn_ref[...] * jnp.abs(dec_ref[...]))


def _pad2(x, rows, cols):
    return jnp.pad(x.astype(f32), ((0, rows - x.shape[0]), (0, cols - x.shape[1])))


def hyena_filters(seq_len, w1, b1, fr1, w2, b2, fr2, w3, decay):
    tn = jnp.arange(seq_len, dtype=f32) / seq_len
    bands = jnp.linspace(HY_BAND_MIN, HY_EMB_BANDS - 1, HY_EMB_BANDS, dtype=f32)
    ang = (2.0 * math.pi) * tn[:, None] * bands
    feats = _pad2(jnp.concatenate([tn[:, None], jnp.cos(ang), jnp.sin(ang)], axis=-1), seq_len, HY_PAD)
    tl = min(seq_len, 512)
    row = lambda v: _pad2(v[None], 1, HY_PAD)
    return pl.pallas_call(
        _filter_kernel,
        out_shape=jax.ShapeDtypeStruct((seq_len, HY_OUT), f32),
        grid=(seq_len // tl,),
        in_specs=[pl.BlockSpec((tl, HY_PAD), lambda i: (i, 0)),
                  pl.BlockSpec((tl, 1), lambda i: (i, 0)),
                  _resident((HY_PAD, HY_PAD)), _resident((1, HY_PAD)), _resident((1, HY_PAD)),
                  _resident((HY_PAD, HY_PAD)), _resident((1, HY_PAD)), _resident((1, HY_PAD)),
                  _resident((HY_PAD, HY_OUT)), _resident((1, HY_OUT))],
        out_specs=pl.BlockSpec((tl, HY_OUT), lambda i: (i, 0)),
        compiler_params=_cp("arbitrary"),
        name="hyena_filters",
    )(feats, tn[:, None], _pad2(w1, HY_PAD, HY_PAD), row(b1), row(fr1), _pad2(w2, HY_PAD, HY_PAD), row(b2), row(fr2),
      _pad2(w3, HY_PAD, HY_OUT), decay.astype(f32).reshape(1, HY_OUT))


def negacyclic_taps(hf):
    seq_len = hf.shape[0]
    causal, anti = hf[:, :2 * D_W], hf[:, 2 * D_W:]
    g = jnp.concatenate([jnp.zeros((1, 2 * D_W), f32), -anti[:0:-1]], axis=0)
    return causal, g


HY_N2 = 128
HY_NJ = 16


def _dft_consts(seq_len):
    n = 2 * seq_len
    n1h = seq_len // HY_N2
    k1 = np.arange(n1h)
    th = 2.0 * np.pi * np.outer(k1 + 0.5, np.arange(n1h)) / (2 * n1h)
    f_a = np.concatenate([np.cos(th), -np.sin(th)], axis=0)
    f_c = np.concatenate([np.cos(th).T, -np.sin(th).T], axis=1)
    n2 = np.arange(HY_N2)
    ph = -2.0 * np.pi * (n2[None, None, :] * (k1[:, None, None] + 0.5) / n + np.outer(n2, n2)[None] / HY_N2)
    gr, gi = np.cos(ph), np.sin(ph)
    g = np.concatenate([np.concatenate([gr, -gi], 2), np.concatenate([gi, gr], 2)], 1)
    hr, hi = np.transpose(gr, (0, 2, 1)), -np.transpose(gi, (0, 2, 1))
    h = np.concatenate([np.concatenate([hr, -hi], 2), np.concatenate([hi, hr], 2)], 1)
    return (jnp.asarray(f_a, f32), jnp.asarray(f_c, f32), jnp.asarray(g, f32), jnp.asarray(h, f32))


def _dot_p(a, b, exact):
    if exact:
        return jnp.dot(a, b, preferred_element_type=f32, precision=lax.Precision.HIGHEST)
    return _bdot(a, b)


def _hy_a_kernel(z_ref, f_ref, a_ref, *, exact):
    f = f_ref[...]
    n1h = f.shape[1]
    for j in range(HY_NJ):
        a = _dot_p(f, z_ref[0, :, j, :], exact)
        a_ref[0, 0, :, j, :] = a[:n1h]
        a_ref[0, 1, :, j, :] = a[n1h:]


def hyena_stage_a(z4, col_block, width, f_a, *, exact=False):
    nb, n1h = z4.shape[0], z4.shape[1]
    return pl.pallas_call(
        functools.partial(_hy_a_kernel, exact=exact),
        out_shape=jax.ShapeDtypeStruct((nb, 2, n1h, HY_N2, width), f32),
        grid=(nb, HY_N2 // HY_NJ),
        in_specs=[pl.BlockSpec((1, n1h, HY_NJ, width), lambda b, j: (b, 0, j, col_block)),
                  _resident((2 * n1h, n1h))],
        out_specs=pl.BlockSpec((1, 2, n1h, HY_NJ, width), lambda b, j: (b, 0, 0, j, 0)),
        compiler_params=_cp("arbitrary", "arbitrary"),
        name="hyena_stage_a",
    )(z4, f_a)


def _hy_spec_kernel(a_ref, g_ref, ks_ref, *, scale):
    k1 = pl.program_id(0)
    sgn = (1 - 2 * (k1 % 2)).astype(f32)
    g = g_ref[0]
    kc = _dot_p(g, jnp.concatenate([a_ref[0, 0, 0], a_ref[0, 1, 0]], axis=0), True)
    kg = _dot_p(g, jnp.concatenate([a_ref[1, 0, 0], a_ref[1, 1, 0]], axis=0), True)
    ks_ref[0, 0] = (kc[:HY_N2] + sgn * kg[HY_N2:]) * scale
    ks_ref[1, 0] = (kc[HY_N2:] - sgn * kg[:HY_N2]) * scale


def hyena_filter_spectrum(a_taps, g_mats, *, seq_len):
    n1h = a_taps.shape[2]
    width = a_taps.shape[-1]
    return pl.pallas_call(
        functools.partial(_hy_spec_kernel, scale=1.0 / seq_len),
        out_shape=jax.ShapeDtypeStruct((2, n1h, HY_N2, width), f32),
        grid=(n1h,),
        in_specs=[pl.BlockSpec((2, 2, 1, HY_N2, width), lambda k: (0, 0, k, 0, 0)),
                  pl.BlockSpec((1, 2 * HY_N2, 2 * HY_N2), lambda k: (k, 0, 0))],
        out_specs=pl.BlockSpec((2, 1, HY_N2, width), lambda k: (0, k, 0, 0)),
        compiler_params=_cp("arbitrary"),
        name="hyena_filter_spectrum",
    )(a_taps, g_mats)


def _hy_b_kernel(a_ref, g_ref, h_ref, ks_ref, c_ref):
    s = _bdot(g_ref[0], jnp.concatenate([a_ref[0, 0, 0], a_ref[0, 1, 0]], axis=0))
    sr, si = s[:HY_N2], s[HY_N2:]
    kr, ki = ks_ref[0, 0], ks_ref[1, 0]
    y = jnp.concatenate([sr * kr - si * ki, sr * ki + si * kr], axis=0)
    c = _bdot(h_ref[0], y)
    c_ref[0, 0, 0] = c[:HY_N2]
    c_ref[0, 1, 0] = c[HY_N2:]


def hyena_stage_b(a, g_mats, h_mats, kspec, order):
    nb, _, n1h, _, width = a.shape
    return pl.pallas_call(
        _hy_b_kernel,
        out_shape=jax.ShapeDtypeStruct(a.shape, f32),
        grid=(n1h, nb),
        in_specs=[pl.BlockSpec((1, 2, 1, HY_N2, width), lambda k, b: (b, 0, k, 0, 0)),
                  pl.BlockSpec((1, 2 * HY_N2, 2 * HY_N2), lambda k, b: (k, 0, 0)),
                  pl.BlockSpec((1, 2 * HY_N2, 2 * HY_N2), lambda k, b: (k, 0, 0)),
                  pl.BlockSpec((2, 1, HY_N2, width), lambda k, b: (0, k, 0, order))],
        out_specs=pl.BlockSpec((1, 2, 1, HY_N2, width), lambda k, b: (b, 0, k, 0, 0)),
        compiler_params=_cp("arbitrary", "arbitrary"),
        name="hyena_stage_b",
    )(a, g_mats, h_mats, kspec)


def _hy_c_kernel(c_ref, f_ref, x_ref, v_ref, bias_ref, o_ref, y_ref):
    f = f_ref[...]
    for j in range(HY_NJ):
        c = jnp.concatenate([c_ref[0, 0, :, j, :], c_ref[0, 1, :, j, :]], axis=0)
        y_ref[:, j, :] = _bdot(f, c)
    v = v_ref[0]
    o_ref[0] = x_ref[0] * (y_ref[...] + bias_ref[...].reshape(1, 1, -1) * v)


def hyena_stage_c(c, f_c, x4, x_block, v4, v_block, bias):
    nb, _, n1h, _, width = c.shape
    blk = lambda col: pl.BlockSpec((1, n1h, HY_NJ, width), lambda b, j: (b, 0, j, col))
    return pl.pallas_call(
        _hy_c_kernel,
        out_shape=jax.ShapeDtypeStruct((nb, n1h, HY_N2, width), f32),
        grid=(nb, HY_N2 // HY_NJ),
        in_specs=[pl.BlockSpec((1, 2, n1h, HY_NJ, width), lambda b, j: (b, 0, 0, j, 0)),
                  _resident((n1h, 2 * n1h)),
                  blk(x_block), blk(v_block),
                  _resident((1, width))],
        out_specs=blk(0),
        scratch_shapes=[pltpu.VMEM((n1h, HY_NJ, width), f32)],
        compiler_params=_cp("arbitrary", "arbitrary"),
        name="hyena_stage_c",
    )(c, f_c, x4, v4, bias)


def hyena_lat(s, hf, bias):
    nb, seq_len, _ = s.shape
    n1h = seq_len // HY_N2
    f_a, f_c, g_mats, h_mats = _dft_consts(seq_len)
    causal, shifted = negacyclic_taps(hf)
    taps4 = jnp.stack([causal, shifted]).reshape(2, n1h, HY_N2, 2 * D_W)
    kspec = hyena_filter_spectrum(hyena_stage_a(taps4, 0, 2 * D_W, f_a, exact=True), g_mats, seq_len=seq_len)
    g_bf, h_bf = g_mats.astype(bf16), h_mats.astype(bf16)
    s4 = s.reshape(nb, n1h, HY_N2, 3 * D_W)
    c0 = hyena_stage_b(hyena_stage_a(s4, 0, D_W, f_a), g_bf, h_bf, kspec, 0)
    z1 = hyena_stage_c(c0, f_c, s4, 1, s4, 0, bias[0:1])
    c1 = hyena_stage_b(hyena_stage_a(z1, 0, D_W, f_a), g_bf, h_bf, kspec, 1)
    z2 = hyena_stage_c(c1, f_c, s4, 2, z1, 0, bias[1:2])
    return z2.reshape(nb, seq_len, D_W)


def _ctx_dft_consts(seq_len):
    n = 2 * seq_len
    k = np.arange(seq_len) + 0.5
    th = 2.0 * np.pi * np.outer(k, np.arange(n)) / n
    fwd_full = np.concatenate([np.cos(th), -np.sin(th)], axis=0)
    inv = np.concatenate([np.cos(th[:, :seq_len]).T, -np.sin(th[:, :seq_len]).T], axis=1) * (2.0 / n)
    return jnp.asarray(fwd_full, f32), jnp.asarray(fwd_full[:, :seq_len], f32), jnp.asarray(inv, f32)


def _ctx_spec_kernel(f_ref, t_ref, o_ref):
    o_ref[...] = _dot_p(f_ref[...], t_ref[...], True)


def _hy_ctx_kernel(v_ref, x1_ref, x2_ref, k0_ref, k1_ref, ff_ref, fi_ref, bias_ref, o_ref):
    ff, fi = ff_ref[...], fi_ref[...]
    n = ff.shape[0] // 2
    z = v_ref[0]
    for x_ref, k_ref, o in ((x1_ref, k0_ref, 0), (x2_ref, k1_ref, 1)):
        s = _bdot(ff, z)
        sr, si = s[:n], s[n:]
        kr, ki = k_ref[:n], k_ref[n:]
        y = _bdot(fi, jnp.concatenate([sr * kr - si * ki, sr * ki + si * kr], axis=0))
        z = x_ref[0] * (y + bias_ref[o:o + 1] * z)
    o_ref[0] = z


HY_CTX_CB = 256


def hyena_ctx(s, hf, bias):
    nb, seq_len, _ = s.shape
    n = 2 * seq_len
    fwd_full, fwd, inv = _ctx_dft_consts(seq_len)
    causal, shifted = negacyclic_taps(hf)
    taps = jnp.concatenate([causal, shifted], axis=0)
    kspec = pl.pallas_call(
        _ctx_spec_kernel,
        out_shape=jax.ShapeDtypeStruct((n, 2 * D_W), f32),
        name="hyena_ctx_spectrum",
    )(fwd_full, taps)
    ncb = D_W // HY_CTX_CB
    col = lambda off: pl.BlockSpec((1, seq_len, HY_CTX_CB), lambda b, j: (b, 0, off + j))
    return pl.pallas_call(
        _hy_ctx_kernel,
        out_shape=jax.ShapeDtypeStruct((nb, seq_len, D_W), f32),
        grid=(nb, ncb),
        in_specs=[col(0), col(ncb), col(2 * ncb),
                  pl.BlockSpec((n, HY_CTX_CB), lambda b, j: (0, j)),
                  pl.BlockSpec((n, HY_CTX_CB), lambda b, j: (0, ncb + j)),
                  _resident((n, seq_len)), _resident((seq_len, n)),
                  pl.BlockSpec((2, HY_CTX_CB), lambda b, j: (0, j))],
        out_specs=pl.BlockSpec((1, seq_len, HY_CTX_CB), lambda b, j: (b, 0, j)),
        compiler_params=_cp("arbitrary", "arbitrary"),
        name="hyena_ctx",
    )(s, s, s, kspec, kspec, fwd.astype(bf16), inv.astype(bf16), bias)


ROW_TILE = 512
ODD_ROWS = 512


def _even_layer(l, xc, xs, mod_c, mod_s, dims, caches, g_pre, g_post, w_in, w_out, rpb, lam_rows, gs, rope_t):
    (nbc, lc), (nbs, ls) = dims
    e = l // 2
    lam_init = 0.8 - 0.6 * math.exp(-0.3 * l)
    lamv = jnp.zeros((8, LANES), f32).at[:4, :DQK].set(lam_rows)
    w_in, w_out = w_in.astype(bf16), w_out.astype(bf16)
    cak, cav, cbk, cbv = caches
    cos_t, sin_t = rope_t
    pc = even_proj(xc, mod_c, g_pre, w_in, cos_t[:ROW_TILE], sin_t[:ROW_TILE], seq_len=lc, rope=False,
                   out_dtype=f32, tm=ROW_TILE)
    ps = even_proj(xs, mod_s, g_pre, w_in, cos_t, sin_t, seq_len=ls, rope=True, out_dtype=bf16, tm=ROW_TILE)
    oa_c, ob_c = ctx_even_attn(pc, lamv, gs, lam_init=lam_init, n_seq=nbc, seq_len=lc)
    tl, tr = na_bias_tables(rpb)
    oa_s = lat_na_attn(ps, cak, cav, tl, tr, e=e, n_seq=nbs, seq_len=ls)
    ob_s = lat_diff_attn(ps, cbk, cbv, lamv, gs, e=e, lam_init=lam_init, n_seq=nbs, seq_len=ls)
    xc = even_out(xc, oa_c, ob_c, mod_c, g_post, w_out, seq_len=lc, tm=ROW_TILE)
    xs = even_out(xs, oa_s, ob_s, mod_s, g_post, w_out, seq_len=ls, tm=ROW_TILE)

    def unblock(lo):
        return jnp.transpose(pc[lo:lo + 4], (1, 0, 2)).reshape(nbc, lc, D // 2)

    new_cache = (unblock(4).reshape(nbc, lc, HA, DH), unblock(8).reshape(nbc, lc, HA, DH),
                 unblock(16).reshape(nbc, lc, HB, 2 * DQK), unblock(20).reshape(nbc, lc, HB, DVB))
    return xc, xs, new_cache


def _odd_mixer(x, mod, h0, g_pre, g_post, w_in, w_out, s5p, glu, hyp):
    nb, seq_len, _ = x.shape
    tt = ODD_ROWS // nb
    u, hy = odd_proj(x, mod, g_pre, w_in, tt=tt)
    conv_w, conv_b, filt, bias = hyp
    s = short_conv(hy, conv_w, conv_b)
    hf = hyena_filters(seq_len, *filt)
    yd = hyena_lat(s, hf, bias) if seq_len > 2 * HY_N2 else hyena_ctx(s, hf, bias)
    w_glu, b_glu, d_skip = glu
    y = d_skip
    fins = []
    u2 = u.reshape(seq_len * nb, C_W)
    for dr in range(2):
        a, bm, cm = s5p[dr]
        y, fin = s5_direction(u2, y, h0[dr], a, bm, cm, w_glu, b_glu, nb=nb, tc=tt, reverse=(dr == 1))
        fins.append(fin)
    x = odd_out(x, y.reshape(seq_len, nb, C_W), yd, mod, g_post, w_out, tt=tt)
    return x, fins


def kernel(x_prompt, x_sample, cache_a_k, cache_a_v, cache_b_k, cache_b_v, state_c_re, state_c_im, c, c_ctx, w_mod, b_mod, g_mix_pre, g_mix_post, g_ffn_pre, g_ffn_post, w_ffn_gate, w_ffn_up, w_ffn_down, w_in_e, w_out_e, na_rpb, lam_q1, lam_k1, lam_q2, lam_k2, g_subln, w_in_o, w_out_o, ssm_lam_re, ssm_lam_im, ssm_log_dt, ssm_b_re, ssm_b_im, ssm_c_re, ssm_c_im, ssm_d, w_glu, b_glu, hy_conv_w, hy_conv_b, hy_w1, hy_b1, hy_fr1, hy_w2, hy_b2, hy_fr2, hy_w3, hy_decay, hy_bias):
    nbc, lc, _ = x_prompt.shape
    nbs, ls, _ = x_sample.shape
    n_even, past = cache_a_k.shape[1], cache_a_k.shape[2]
    dims = ((nbc, lc), (nbs, ls))
    cond = jnp.concatenate([c_ctx[None], c, jnp.zeros((16 - 1 - nbs, D), f32)], axis=0)
    mod_all = modulation(cond, w_mod, b_mod)
    xc = x_prompt.reshape(nbc * lc, D)
    xs = x_sample.reshape(nbs * ls, D)
    caches = tuple(t.reshape(nbs, n_even, past, D // 2) for t in (cache_a_k, cache_a_v, cache_b_k, cache_b_v))
    rope_t = rope_tables(ls)
    new_caches, new_sre, new_sim = [], [], []
    row = lambda v: v[None]
    for l in range(DEPTH):
        mod_c = mod_all[l, 0:1].reshape(1, 1, 6 * D)
        mod_s = mod_all[l, 1:1 + nbs].reshape(nbs, 1, 6 * D)
        if l % 2 == 0:
            e = l // 2
            lam_rows = jnp.stack([lam_q1[e], lam_k1[e], lam_q2[e], lam_k2[e]])
            xc, xs, nc = _even_layer(l, xc, xs, mod_c, mod_s, dims, caches, row(g_mix_pre[l]), row(g_mix_post[l]),
                                     w_in_e[e], w_out_e[e], na_rpb[e], lam_rows, row(g_subln[e]), rope_t)
            new_caches.append(nc)
        else:
            o = l // 2
            s5p = [s5_prepare(ssm_lam_re[o, dr], ssm_lam_im[o, dr], ssm_log_dt[o, dr], ssm_b_re[o, dr], ssm_b_im[o, dr],
                              ssm_c_re[o, dr], ssm_c_im[o, dr]) for dr in range(2)]
            glu = (w_glu[o].astype(bf16), row(b_glu[o]), row(ssm_d[o]))
            filt = (hy_w1[o], hy_b1[o], hy_fr1[o], hy_w2[o], hy_b2[o], hy_fr2[o], hy_w3[o], hy_decay[o])
            hyp = (hy_conv_w[o], row(hy_conv_b[o]), filt, hy_bias[o])
            w_in, w_out = w_in_o[o].astype(bf16), w_out_o[o].astype(bf16)
            zero_state = jnp.zeros((nbc, 2 * S5_STATES), f32)
            xc3, fins = _odd_mixer(xc.reshape(nbc, lc, D), jnp.broadcast_to(mod_c, (nbc, 1, 6 * D)), (zero_state, zero_state),
                                   row(g_mix_pre[l]), row(g_mix_post[l]), w_in, w_out, s5p, glu, hyp)
            h0 = [jnp.concatenate([state_c_re[:, o, dr].reshape(nbs, S5_STATES), state_c_im[:, o, dr].reshape(nbs, S5_STATES)],
                                  axis=1) for dr in range(2)]
            xs3, _ = _odd_mixer(xs.reshape(nbs, ls, D), mod_s, h0, row(g_mix_pre[l]), row(g_mix_post[l]), w_in, w_out,
                                s5p, glu, hyp)
            xc, xs = xc3.reshape(nbc * lc, D), xs3.reshape(nbs * ls, D)
            new_sre.append(jnp.stack([f[:, :S5_STATES].reshape(nbc, C_G, C_P) for f in fins], axis=1))
            new_sim.append(jnp.stack([f[:, S5_STATES:].reshape(nbc, C_G, C_P) for f in fins], axis=1))
        wg, wu, wd = w_ffn_gate[l].astype(bf16), w_ffn_up[l].astype(bf16), w_ffn_down[l].astype(bf16)
        xc = ffn(xc, mod_c, row(g_ffn_pre[l]), row(g_ffn_post[l]), wg, wu, wd, seq_len=lc, tm=ROW_TILE)
        xs = ffn(xs, mod_s, row(g_ffn_pre[l]), row(g_ffn_post[l]), wg, wu, wd, seq_len=ls, tm=ROW_TILE)
    new_a_k, new_a_v, new_b_k, new_b_v = (jnp.stack([nc[i] for nc in new_caches], axis=1) for i in range(4))
    return (xc.reshape(nbc, lc, D), xs.reshape(nbs, ls, D), new_a_k, new_a_v, new_b_k, new_b_v,
            jnp.stack(new_sre, axis=1), jnp.stack(new_sim, axis=1))
```

```python
import functools
import math

import numpy as np
import jax
import jax.numpy as jnp
from jax import lax
from jax.experimental import pallas as pl
from jax.experimental.pallas import tpu as pltpu

f32 = jnp.float32
bf16 = jnp.bfloat16

D = 1024
DEPTH = 4
D_FF = 2816
GRID_W = 64
HA, DH = 8, 64
HB, DQK, DVB = 4, 64, 128
NA_KH, NA_KW = 8, 16
C_W, C_G, C_GH, C_P = 512, 32, 16, 64
D_W = 512
HY_EMB_BANDS = 16
HY_BAND_MIN = 1e-4
ROPE_BASE = 10000.0
EPS = 1e-6
DT_MIN = 1e-3
DT_MAX = 1e-1
NEG = -1e30
LOG2E = math.log2(math.e)
Q_PRESCALE = DH ** -0.5 * LOG2E

LANES = 128
VMEM_LIMIT = 56 * 1024 * 1024


def _cp(*sem):
    return pltpu.CompilerParams(dimension_semantics=sem, vmem_limit_bytes=VMEM_LIMIT)


def _bdot(a, b):
    return jnp.dot(a.astype(bf16), b.astype(bf16), preferred_element_type=f32)


def _bdot_t(a, b):
    return lax.dot_general(a.astype(bf16), b.astype(bf16), (((1,), (1,)), ((), ())),
                           preferred_element_type=f32)


def _rms(x, g):
    return x * lax.rsqrt(jnp.mean(x * x, axis=-1, keepdims=True) + EPS) * g


def _silu(x):
    return x * (1.0 / (1.0 + jnp.exp(-x)))


def _resident(shape):
    nd = len(shape)
    return pl.BlockSpec(shape, lambda *_: (0,) * nd, pipeline_mode=pl.Buffered(1))


def _mod_kernel(c_ref, w_ref, b_ref, o_ref):
    o_ref[0] = _bdot(_silu(c_ref[...]), w_ref[0]) + b_ref[0]


def modulation(cond, w_mod, b_mod):
    nb = 4
    cb = 6 * D // nb
    return pl.pallas_call(
        _mod_kernel,
        out_shape=jax.ShapeDtypeStruct((DEPTH, 16, 6 * D), f32),
        grid=(DEPTH, nb),
        in_specs=[pl.BlockSpec((16, D), lambda l, j: (0, 0)),
                  pl.BlockSpec((1, D, cb), lambda l, j: (l, 0, j)),
                  pl.BlockSpec((1, 1, cb), lambda l, j: (l, 0, j))],
        out_specs=pl.BlockSpec((1, 16, cb), lambda l, j: (l, 0, j)),
        compiler_params=_cp("arbitrary", "arbitrary"),
        name="modulation",
    )(cond, w_mod, b_mod.reshape(DEPTH, 1, 6 * D))


def _mod_slice(m_ref, k):
    return m_ref[0, :, k * D:(k + 1) * D]


def _mod_map(n_seq, tiles_per_seq):
    if n_seq == 1:
        return lambda i: (0, 0, 0)
    return lambda i: (i // tiles_per_seq, 0, 0)


def _even_proj_kernel(x_ref, m_ref, g_ref, w_ref, cs_ref, sn_ref, o_ref, *, rope):
    h = _rms(x_ref[...], g_ref[...]) * (1.0 + _mod_slice(m_ref, 1)) + _mod_slice(m_ref, 0)
    p = _bdot(h, w_ref[...])
    if rope:
        lane = lax.broadcasted_iota(jnp.int32, (1, LANES), 1)
        first = (lane % 32) < 16
        c, s = cs_ref[...], sn_ref[...]
    for j in range(24):
        blk = p[:, j * LANES:(j + 1) * LANES]
        if rope and 12 <= j < 20:
            partner = jnp.where(first, pltpu.roll(blk, LANES - 16, axis=1), pltpu.roll(blk, 16, axis=1))
            blk = blk * c + partner * s
        if j < 4 or 12 <= j < 16:
            blk = blk * Q_PRESCALE
        o_ref[j] = blk.astype(o_ref.dtype)


def even_proj(x, mod, g, w, cos_t, sin_t, *, seq_len, rope, out_dtype, tm):
    rows = x.shape[0]
    n_seq = mod.shape[0]
    tps = seq_len // tm if n_seq > 1 else 1
    tbl_map = (lambda i: (i % tps, 0)) if rope else (lambda i: (0, 0))
    return pl.pallas_call(
        functools.partial(_even_proj_kernel, rope=rope),
        out_shape=jax.ShapeDtypeStruct((24, rows, LANES), out_dtype),
        grid=(rows // tm,),
        in_specs=[pl.BlockSpec((tm, D), lambda i: (i, 0)),
                  pl.BlockSpec((1, 1, 6 * D), _mod_map(n_seq, tps)),
                  _resident((1, D)),
                  _resident((D, 3 * D)),
                  pl.BlockSpec((tm, LANES), tbl_map),
                  pl.BlockSpec((tm, LANES), tbl_map)],
        out_specs=pl.BlockSpec((24, tm, LANES), lambda i: (0, i, 0)),
        compiler_params=_cp("arbitrary"),
        name="even_proj",
    )(x, mod, g, w, cos_t, sin_t)


def _even_out_kernel(x_ref, oa_ref, ob_ref, m_ref, g_ref, w_ref, o_ref):
    w = w_ref[...]
    o = _bdot(oa_ref[...], w[:D // 2]) + _bdot(ob_ref[...], w[D // 2:])
    o_ref[...] = x_ref[...] + _mod_slice(m_ref, 2) * _rms(o, g_ref[...])


def even_out(x, oa, ob, mod, g, w, *, seq_len, tm):
    rows = x.shape[0]
    n_seq = mod.shape[0]
    tps = seq_len // tm if n_seq > 1 else 1
    return pl.pallas_call(
        _even_out_kernel,
        out_shape=jax.ShapeDtypeStruct((rows, D), f32),
        grid=(rows // tm,),
        in_specs=[pl.BlockSpec((tm, D), lambda i: (i, 0)),
                  pl.BlockSpec((tm, D // 2), lambda i: (i, 0)),
                  pl.BlockSpec((tm, D // 2), lambda i: (i, 0)),
                  pl.BlockSpec((1, 1, 6 * D), _mod_map(n_seq, tps)),
                  _resident((1, D)),
                  _resident((D, D))],
        out_specs=pl.BlockSpec((tm, D), lambda i: (i, 0)),
        compiler_params=_cp("arbitrary"),
        name="even_out",
    )(x, oa, ob, mod, g, w)


FF_CHUNK = D_FF // 2


def _ffn_kernel(x_ref, m_ref, gpre_ref, gpost_ref, wg_ref, wu_ref, wd_ref, o_ref):
    x = x_ref[...]
    h = (_rms(x, gpre_ref[...]) * (1.0 + _mod_slice(m_ref, 4)) + _mod_slice(m_ref, 3)).astype(bf16)
    acc = None
    for c in range(D_FF // FF_CHUNK):
        sl = slice(c * FF_CHUNK, (c + 1) * FF_CHUNK)
        a = _silu(_bdot(h, wg_ref[:, sl])) * _bdot(h, wu_ref[:, sl])
        part = _bdot(a, wd_ref[sl, :])
        acc = part if acc is None else acc + part
    o_ref[...] = x + _mod_slice(m_ref, 5) * _rms(acc, gpost_ref[...])


def ffn(x, mod, gpre, gpost, wg, wu, wd, *, seq_len, tm):
    rows = x.shape[0]
    n_seq = mod.shape[0]
    tps = seq_len // tm if n_seq > 1 else 1
    return pl.pallas_call(
        _ffn_kernel,
        out_shape=jax.ShapeDtypeStruct((rows, D), f32),
        grid=(rows // tm,),
        in_specs=[pl.BlockSpec((tm, D), lambda i: (i, 0)),
                  pl.BlockSpec((1, 1, 6 * D), _mod_map(n_seq, tps)),
                  _resident((1, D)),
                  _resident((1, D)),
                  _resident((D, D_FF)),
                  _resident((D, D_FF)),
                  _resident((D_FF, D))],
        out_specs=pl.BlockSpec((tm, D), lambda i: (i, 0)),
        compiler_params=_cp("arbitrary"),
        name="ffn",
    )(x, mod, gpre, gpost, wg, wu, wd)


def rope_tables(seq_len):
    lane = np.arange(LANES)
    d = lane % DQK
    nf = DQK // 4
    inv = ROPE_BASE ** (-jnp.arange(nf, dtype=f32) / nf)
    t = jnp.arange(seq_len)
    pos = jnp.where((d < DQK // 2)[None, :], (t // GRID_W)[:, None], (t % GRID_W)[:, None]).astype(f32)
    ang = pos * inv[d % nf][None, :]
    sign = np.where((d % (DQK // 2)) < nf, -1.0, 1.0).astype(np.float32)
    return jnp.cos(ang), jnp.sin(ang) * sign[None, :]


def _diff_lambda(lam_ref, lam_init):
    l = lam_ref[...]
    a1 = jnp.sum(l[0:1] * l[1:2], axis=1, keepdims=True)
    a2 = jnp.sum(l[2:3] * l[3:4], axis=1, keepdims=True)
    return jnp.exp(a1) - jnp.exp(a2) + lam_init


def _half_masks():
    lane = lax.broadcasted_iota(jnp.int32, (1, LANES), 1)
    lo = lane < (LANES // 2)
    return lo, jnp.logical_not(lo)


def _softmax_parts(s):
    m = jnp.max(s, axis=-1, keepdims=True)
    e = jnp.exp2(s - m)
    return e, jnp.sum(e, axis=-1, keepdims=True)


def _ctx_attn_kernel(p_ref, lam_ref, gs_ref, oa_ref, ob_ref, *, lam_init):
    lo, hi = _half_masks()
    lam = _diff_lambda(lam_ref, lam_init)
    for pr in range(HA // 2):
        q, k, v = p_ref[pr], p_ref[4 + pr], p_ref[8 + pr]
        outs = []
        for msk in (lo, hi):
            e, l = _softmax_parts(_bdot_t(jnp.where(msk, q, 0.0), k))
            outs.append(_bdot(e, v) / l)
        oa_ref[:, pr * LANES:(pr + 1) * LANES] = jnp.where(lo, outs[0], outs[1])
    for h in range(HB):
        q, k, v = p_ref[12 + h], p_ref[16 + h], p_ref[20 + h]
        ps = []
        for msk in (lo, hi):
            e, l = _softmax_parts(_bdot_t(jnp.where(msk, q, 0.0), k))
            ps.append(e / l)
        o = _bdot(ps[0] - lam * ps[1], v)
        ob_ref[:, h * LANES:(h + 1) * LANES] = _rms(o, gs_ref[...]) * (1.0 - lam_init)


def ctx_even_attn(p, lamv, gs, *, lam_init, n_seq, seq_len):
    rows = n_seq * seq_len
    return pl.pallas_call(
        functools.partial(_ctx_attn_kernel, lam_init=lam_init),
        out_shape=(jax.ShapeDtypeStruct((rows, D // 2), f32), jax.ShapeDtypeStruct((rows, D // 2), f32)),
        grid=(n_seq,),
        in_specs=[pl.BlockSpec((24, seq_len, LANES), lambda b: (0, b, 0)),
                  _resident((8, LANES)),
                  _resident((1, DVB))],
        out_specs=(pl.BlockSpec((seq_len, D // 2), lambda b: (b, 0)),
                   pl.BlockSpec((seq_len, D // 2), lambda b: (b, 0))),
        compiler_params=_cp("arbitrary"),
        name="ctx_even_attn",
    )(p, lamv, gs)


NA_QROWS = 4
NA_KROWS = NA_QROWS + NA_KH
NA_TBL = 2 * NA_KH


def na_bias_tables(rpb):
    cols = np.arange(GRID_W)
    cs = np.clip(cols - NA_KW // 2, 0, GRID_W - NA_KW)
    cp = cols[None, :]
    inwin = (cp >= cs[:, None]) & (cp < cs[:, None] + NA_KW)
    dcol = np.clip(cp - cols[:, None] + (NA_KW - 1), 0, 2 * NA_KW - 2)
    tm = jnp.where(inwin[None, None], rpb[:, :, dcol] * LOG2E, NEG)
    tm = jnp.concatenate([tm, jnp.full((HA, 1, GRID_W, GRID_W), NEG, f32)], axis=1)
    z = jnp.zeros_like(tm)
    return jnp.concatenate([tm, z], axis=-1), jnp.concatenate([z, tm], axis=-1)


def _na_kernel(q_ref, k_ref, v_ref, kc_ref, vc_ref, tl_ref, tr_ref, o_ref, *, n_rows):
    i = pl.program_id(2)
    ws = jnp.clip(NA_QROWS * i - NA_KH // 2, 0, n_rows - NA_KROWS)
    start = pl.multiple_of(ws * GRID_W, GRID_W)
    q = q_ref[0]
    kl = k_ref[0, pl.ds(start, NA_KROWS * GRID_W), :]
    vl = v_ref[0, pl.ds(start, NA_KROWS * GRID_W), :]
    kc = kc_ref[0, 0].astype(bf16)
    vc = vc_ref[0, 0].astype(bf16)
    lo, hi = _half_masks()
    outs = []
    for hh, msk in enumerate((lo, hi)):
        qm = jnp.where(msk, q, jnp.zeros_like(q))
        s_loc = _bdot_t(qm, kl)
        s_ctx = _bdot_t(qm, kc)
        bias_rows = []
        for qr in range(NA_QROWS):
            r = NA_QROWS * i + qr
            rs = jnp.clip(r - NA_KH // 2, 0, n_rows - NA_KH)

            def tbl_idx(kr):
                rp = ws + kr
                valid = jnp.logical_and(rp >= rs, rp < rs + NA_KH)
                return jnp.where(valid, rp - r + (NA_KH - 1), NA_TBL - 1)

            tiles = [tl_ref[hh, tbl_idx(2 * kp)] + tr_ref[hh, tbl_idx(2 * kp + 1)]
                     for kp in range(NA_KROWS // 2)]
            bias_rows.append(jnp.concatenate(tiles, axis=1))
        s_loc = s_loc + jnp.concatenate(bias_rows, axis=0)
        m = jnp.maximum(jnp.max(s_loc, axis=-1, keepdims=True), jnp.max(s_ctx, axis=-1, keepdims=True))
        e_loc = jnp.exp2(s_loc - m)
        e_ctx = jnp.exp2(s_ctx - m)
        l = jnp.sum(e_loc, axis=-1, keepdims=True) + jnp.sum(e_ctx, axis=-1, keepdims=True)
        outs.append((_bdot(e_loc, vl) + _bdot(e_ctx, vc)) / l)
    o_ref[...] = jnp.where(lo, outs[0], outs[1])


def lat_na_attn(p, cache_k, cache_v, tl, tr, *, e, n_seq, seq_len):
    n_rows = seq_len // GRID_W
    steps = n_rows // NA_QROWS
    tq = NA_QROWS * GRID_W
    past = cache_k.shape[2]
    return pl.pallas_call(
        functools.partial(_na_kernel, n_rows=n_rows),
        out_shape=jax.ShapeDtypeStruct((n_seq * seq_len, D // 2), f32),
        grid=(n_seq, HA // 2, steps),
        in_specs=[pl.BlockSpec((1, tq, LANES), lambda b, pr, i: (pr, b * steps + i, 0)),
                  pl.BlockSpec((1, seq_len, LANES), lambda b, pr, i: (4 + pr, b, 0)),
                  pl.BlockSpec((1, seq_len, LANES), lambda b, pr, i: (8 + pr, b, 0)),
                  pl.BlockSpec((1, 1, past, LANES), lambda b, pr, i: (b, e, 0, pr)),
                  pl.BlockSpec((1, 1, past, LANES), lambda b, pr, i: (b, e, 0, pr)),
                  pl.BlockSpec((2, NA_TBL, GRID_W, LANES), lambda b, pr, i: (pr, 0, 0, 0)),
                  pl.BlockSpec((2, NA_TBL, GRID_W, LANES), lambda b, pr, i: (pr, 0, 0, 0))],
        out_specs=pl.BlockSpec((tq, LANES), lambda b, pr, i: (b * steps + i, pr)),
        compiler_params=_cp("arbitrary", "arbitrary", "arbitrary"),
        name="lat_na_attn",
    )(p, p, p, cache_k, cache_v, tl, tr)


DIFF_TQ = 512
DIFF_TK = 512


def _diff_kernel(q_ref, k_ref, v_ref, kc_ref, vc_ref, lam_ref, gs_ref, o_ref, *, lam_init, seq_len):
    lo, hi = _half_masks()
    q = q_ref[0]
    qms = [jnp.where(msk, q, jnp.zeros_like(q)) for msk in (lo, hi)]
    tq = q.shape[0]

    def absorb(kb, vb, carry):
        new = []
        for qm, (m, l, a) in zip(qms, carry):
            s = _bdot_t(qm, kb)
            mn = jnp.maximum(m, jnp.max(s, axis=-1, keepdims=True))
            alpha = jnp.exp2(m - mn)
            e = jnp.exp2(s - mn)
            new.append((mn, alpha * l + jnp.sum(e, axis=-1, keepdims=True), alpha * a + _bdot(e, vb)))
        return tuple(new)

    def body(j, carry):
        off = pl.multiple_of(j * DIFF_TK, DIFF_TK)
        return absorb(k_ref[0, pl.ds(off, DIFF_TK), :], v_ref[0, pl.ds(off, DIFF_TK), :], carry)

    one = (jnp.full((tq, 1), NEG, f32), jnp.zeros((tq, 1), f32), jnp.zeros((tq, LANES), f32))
    carry = lax.fori_loop(0, seq_len // DIFF_TK, body, (one, one))
    (_, l0, a0), (_, l1, a1) = absorb(kc_ref[0, 0].astype(bf16), vc_ref[0, 0].astype(bf16), carry)
    o = a0 / l0 - _diff_lambda(lam_ref, lam_init) * (a1 / l1)
    o_ref[...] = _rms(o, gs_ref[...]) * (1.0 - lam_init)


def lat_diff_attn(p, cache_k, cache_v, lamv, gs, *, e, lam_init, n_seq, seq_len):
    steps = seq_len // DIFF_TQ
    past = cache_k.shape[2]
    return pl.pallas_call(
        functools.partial(_diff_kernel, lam_init=lam_init, seq_len=seq_len),
        out_shape=jax.ShapeDtypeStruct((n_seq * seq_len, D // 2), f32),
        grid=(n_seq, HB, steps),
        in_specs=[pl.BlockSpec((1, DIFF_TQ, LANES), lambda b, h, i: (12 + h, b * steps + i, 0)),
                  pl.BlockSpec((1, seq_len, LANES), lambda b, h, i: (16 + h, b, 0)),
                  pl.BlockSpec((1, seq_len, LANES), lambda b, h, i: (20 + h, b, 0)),
                  pl.BlockSpec((1, 1, past, LANES), lambda b, h, i: (b, e, 0, h)),
                  pl.BlockSpec((1, 1, past, LANES), lambda b, h, i: (b, e, 0, h)),
                  _resident((8, LANES)),
                  _resident((1, DVB))],
        out_specs=pl.BlockSpec((DIFF_TQ, LANES), lambda b, h, i: (b * steps + i, h)),
        compiler_params=_cp("arbitrary", "arbitrary", "arbitrary"),
        name="lat_diff_attn",
    )(p, p, p, cache_k, cache_v, lamv, gs)


def _odd_proj_kernel(x_ref, m_ref, g_ref, w_ref, u_ref, hy_ref, *, nb):
    m = m_ref[...]
    tt = x_ref.shape[1]
    h = _rms(x_ref[...], g_ref[...]) * (1.0 + m[:, :, D:2 * D]) + m[:, :, :D]
    p = _bdot(h.reshape(nb * tt, D), w_ref[...]).reshape(nb, tt, C_W + 3 * D_W)
    hy_ref[...] = p[:, :, C_W:]
    for b in range(nb):
        u_ref[:, b, :] = p[b, :, :C_W]


def odd_proj(x, mod, g, w, *, tt):
    nb, seq_len, _ = x.shape
    return pl.pallas_call(
        functools.partial(_odd_proj_kernel, nb=nb),
        out_shape=(jax.ShapeDtypeStruct((seq_len, nb, C_W), f32),
                   jax.ShapeDtypeStruct((nb, seq_len, 3 * D_W), f32)),
        grid=(seq_len // tt,),
        in_specs=[pl.BlockSpec((nb, tt, D), lambda i: (0, i, 0)),
                  _resident((nb, 1, 6 * D)),
                  _resident((1, D)),
                  _resident((D, C_W + 3 * D_W))],
        out_specs=(pl.BlockSpec((tt, nb, C_W), lambda i: (i, 0, 0)),
                   pl.BlockSpec((nb, tt, 3 * D_W), lambda i: (0, i, 0))),
        compiler_params=_cp("arbitrary"),
        name="odd_proj",
    )(x, mod, g, w)


def _odd_out_kernel(x_ref, yc_ref, yd_ref, m_ref, g_ref, w_ref, o_ref, *, nb):
    tt = x_ref.shape[1]
    yc = jnp.concatenate([yc_ref[:, b, :] for b in range(nb)], axis=0)
    o = _bdot(yc, w_ref[:C_W]) + _bdot(yd_ref[...].reshape(nb * tt, D_W), w_ref[C_W:])
    o_ref[...] = x_ref[...] + m_ref[...][:, :, 2 * D:3 * D] * _rms(o, g_ref[...]).reshape(nb, tt, D)


def odd_out(x, yc, yd, mod, g, w, *, tt):
    nb, seq_len, _ = x.shape
    return pl.pallas_call(
        functools.partial(_odd_out_kernel, nb=nb),
        out_shape=jax.ShapeDtypeStruct((nb, seq_len, D), f32),
        grid=(seq_len // tt,),
        in_specs=[pl.BlockSpec((nb, tt, D), lambda i: (0, i, 0)),
                  pl.BlockSpec((tt, nb, C_W), lambda i: (i, 0, 0)),
                  pl.BlockSpec((nb, tt, D_W), lambda i: (0, i, 0)),
                  _resident((nb, 1, 6 * D)),
                  _resident((1, D)),
                  _resident((D, D))],
        out_specs=pl.BlockSpec((nb, tt, D), lambda i: (0, i, 0)),
        compiler_params=_cp("arbitrary"),
        name="odd_out",
    )(x, yc, yd, mod, g, w)


S5_STATES = C_G * C_P
S5_SB = 512


def s5_prepare(lam_re, lam_im, log_dt, b_re, b_im, c_re, c_im):
    lr, li = lam_re.astype(f32), lam_im.astype(f32)
    dt = jnp.exp(log_dt.astype(f32))[:, None]
    mag = jnp.exp(lr * dt)
    ab_re, ab_im = mag * jnp.cos(li * dt), mag * jnp.sin(li * dt)
    den = lr * lr + li * li
    k_re = ((ab_re - 1.0) * lr + ab_im * li) / den
    k_im = (ab_im * lr - (ab_re - 1.0) * li) / den
    br, bi = b_re.astype(f32), b_im.astype(f32)
    bb_re = k_re[..., None] * br - k_im[..., None] * bi
    bb_im = k_re[..., None] * bi + k_im[..., None] * br
    eye = jnp.eye(C_G // 2, dtype=f32)

    def in_blocks(bb):
        return jnp.einsum('agph,gk->aghkp', bb.reshape(2, C_G // 2, C_P, C_GH), eye).reshape(2, C_W // 2, S5_STATES // 2)

    def out_blocks(c):
        return jnp.einsum('aghp,gk->agpkh', c.reshape(2, C_G // 2, C_GH, C_P), eye).reshape(2, S5_STATES // 2, C_W // 2)

    bm = jnp.concatenate([in_blocks(bb_re), in_blocks(bb_im)], axis=-1).astype(bf16)
    cm = jnp.concatenate([out_blocks(c_re.astype(f32)), -out_blocks(c_im.astype(f32))], axis=1).astype(bf16)
    a = jnp.concatenate([ab_re.reshape(1, S5_STATES), ab_im.reshape(1, S5_STATES)], axis=1)
    return a, bm, cm


def _gelu_tanh(x):
    return 0.5 * x * (1.0 + jnp.tanh(math.sqrt(2.0 / math.pi) * (x + 0.044715 * (x * x * x))))


def _s5_kernel(u_ref, add_ref, h0_ref, a_ref, bm_ref, cm_ref, wg_ref, bg_ref, y_ref, fin_ref, xs_ref, car_ref,
               *, nb, tc, reverse):
    i = pl.program_id(0)
    half_s = S5_STATES // 2

    @pl.when(i == 0)
    def _():
        car_ref[...] = h0_ref[...]

    u = u_ref[...]
    for hf in range(2):
        r = _bdot(u[:, hf * (C_W // 2):(hf + 1) * (C_W // 2)], bm_ref[hf])
        xs_ref[:, hf * half_s:(hf + 1) * half_s] = r[:, :half_s]
        xs_ref[:, S5_STATES + hf * half_s:S5_STATES + (hf + 1) * half_s] = r[:, half_s:]

    for sb in range(S5_STATES // S5_SB):
        cre = slice(sb * S5_SB, (sb + 1) * S5_SB)
        cim = slice(S5_STATES + sb * S5_SB, S5_STATES + (sb + 1) * S5_SB)
        ar = jnp.broadcast_to(a_ref[:, cre], (nb, S5_SB))
        ai = jnp.broadcast_to(a_ref[:, cim], (nb, S5_SB))

        def body(s, carry):
            xr, xi = carry
            t = (tc - 1 - s) if reverse else s
            r0 = pl.multiple_of(t * nb, nb)
            nr = ar * xr - ai * xi + xs_ref[pl.ds(r0, nb), cre]
            ni = ar * xi + ai * xr + xs_ref[pl.ds(r0, nb), cim]
            xs_ref[pl.ds(r0, nb), cre] = nr
            xs_ref[pl.ds(r0, nb), cim] = ni
            return nr, ni

        xr, xi = lax.fori_loop(0, tc, body, (car_ref[:, cre], car_ref[:, cim]), unroll=4)
        car_ref[:, cre] = xr
        car_ref[:, cim] = xi

    ys = []
    for hf in range(2):
        xre = xs_ref[:, hf * half_s:(hf + 1) * half_s]
        xim = xs_ref[:, S5_STATES + hf * half_s:S5_STATES + (hf + 1) * half_s]
        ys.append(_bdot(xre, cm_ref[hf, :half_s]) + _bdot(xim, cm_ref[hf, half_s:]))
    y = jnp.concatenate(ys, axis=1)
    if reverse:
        y = y + add_ref[...]
        g = _gelu_tanh(y)
        y = g * (1.0 / (1.0 + jnp.exp(-(_bdot(g, wg_ref[...]) + bg_ref[...]))))
    else:
        y = y + add_ref[...] * u
    y_ref[...] = y
    fin_ref[...] = car_ref[...]


def s5_direction(u, add, h0, a, bm, cm, w_glu, b_glu, *, nb, tc, reverse):
    rows = u.shape[0]
    rb = tc * nb
    nc = rows // rb
    cmap = (lambda i: (nc - 1 - i, 0)) if reverse else (lambda i: (i, 0))
    add_spec = pl.BlockSpec((rb, C_W), cmap) if reverse else _resident((1, C_W))
    return pl.pallas_call(
        functools.partial(_s5_kernel, nb=nb, tc=tc, reverse=reverse),
        out_shape=(jax.ShapeDtypeStruct((rows, C_W), f32), jax.ShapeDtypeStruct((nb, 2 * S5_STATES), f32)),
        grid=(nc,),
        in_specs=[pl.BlockSpec((rb, C_W), cmap),
                  add_spec,
                  _resident((nb, 2 * S5_STATES)),
                  _resident((1, 2 * S5_STATES)),
                  _resident((2, C_W // 2, S5_STATES)),
                  _resident((2, S5_STATES, C_W // 2)),
                  _resident((C_W, C_W)),
                  _resident((1, C_W))],
        out_specs=(pl.BlockSpec((rb, C_W), cmap),
                   pl.BlockSpec((nb, 2 * S5_STATES), lambda i: (0, 0))),
        scratch_shapes=[pltpu.VMEM((rb, 2 * S5_STATES), f32), pltpu.VMEM((nb, 2 * S5_STATES), f32)],
        compiler_params=_cp("arbitrary"),
        name="s5_rev" if reverse else "s5_fwd",
    )(u, add, h0, a, bm, cm, w_glu, b_glu)


def _sconv_kernel(x_ref, w_ref, b_ref, o_ref):
    x = x_ref[0]
    n = x.shape[0]
    row = lax.broadcasted_iota(jnp.int32, (n, 1), 0)
    prev = jnp.where(row == 0, 0.0, pltpu.roll(x, 1, axis=0))
    nxt = jnp.where(row == n - 1, 0.0, pltpu.roll(x, n - 1, axis=0))
    w = w_ref[...]
    o_ref[0] = prev * w[0:1] + x * w[1:2] + nxt * w[2:3] + b_ref[...]


SCONV_BLOCK_ELEMS = 512 * 1024


def short_conv(x, w, b):
    nb, seq_len, ch = x.shape
    cw = min(ch, max(LANES, SCONV_BLOCK_ELEMS // seq_len // LANES * LANES))
    while ch % cw:
        cw -= LANES
    return pl.pallas_call(
        _sconv_kernel,
        out_shape=jax.ShapeDtypeStruct(x.shape, f32),
        grid=(nb, ch // cw),
        in_specs=[pl.BlockSpec((1, seq_len, cw), lambda b_, j: (b_, 0, j)),
                  pl.BlockSpec((3, cw), lambda b_, j: (0, j)),
                  pl.BlockSpec((1, cw), lambda b_, j: (0, j))],
        out_specs=pl.BlockSpec((1, seq_len, cw), lambda b_, j: (b_, 0, j)),
        compiler_params=_cp("arbitrary", "arbitrary"),
        name="short_conv",
    )(x, w, b)


HY_PAD = LANES
HY_OUT = 4 * D_W


def _filter_kernel(fc_ref, tc_ref, fa_ref, ta_ref, sa_ref, w1_ref, b1_ref, fr1_ref, w2_ref, b2_ref, fr2_ref, w3_ref,
                   dec_ref, oc_ref, oa_ref):
    half = HY_OUT // 2

    def hidden(f_ref):
        z = jnp.sin(fr1_ref[...] * (_bdot(f_ref[...], w1_ref[...]) + b1_ref[...]))
        return jnp.sin(fr2_ref[...] * (_bdot(z, w2_ref[...]) + b2_ref[...]))

    oc_ref[...] = _bdot(hidden(fc_ref), w3_ref[:, :half]) * jnp.exp(-tc_ref[...] * jnp.abs(dec_ref[:, :half]))
    oa_ref[...] = sa_ref[...] * (_bdot(hidden(fa_ref), w3_ref[:, half:]) * jnp.exp(-ta_ref[...] * jnp.abs(dec_ref[:, half:])))


def _pad2(x, rows, cols):
    return jnp.pad(x.astype(f32), ((0, rows - x.shape[0]), (0, cols - x.shape[1])))


def _filter_features(pos, seq_len):
    tn = pos.astype(f32) / seq_len
    bands = jnp.linspace(HY_BAND_MIN, HY_EMB_BANDS - 1, HY_EMB_BANDS, dtype=f32)
    ang = (2.0 * math.pi) * tn[:, None] * bands
    feats = jnp.concatenate([tn[:, None], jnp.cos(ang), jnp.sin(ang)], axis=-1)
    return _pad2(feats, seq_len, HY_PAD), tn[:, None]


def hyena_filters(seq_len, w1, b1, fr1, w2, b2, fr2, w3, decay):
    m = jnp.arange(seq_len)
    feats_c, tn_c = _filter_features(m, seq_len)
    feats_a, tn_a = _filter_features((seq_len - m) % seq_len, seq_len)
    sign_a = jnp.where(m == 0, 0.0, -1.0).astype(f32)[:, None]
    tl = min(seq_len, 512)
    row = lambda v: _pad2(v[None], 1, HY_PAD)
    rows = lambda w: pl.BlockSpec((tl, w), lambda i: (i, 0))
    return pl.pallas_call(
        _filter_kernel,
        out_shape=(jax.ShapeDtypeStruct((seq_len, HY_OUT // 2), f32), jax.ShapeDtypeStruct((seq_len, HY_OUT // 2), f32)),
        grid=(seq_len // tl,),
        in_specs=[rows(HY_PAD), rows(1), rows(HY_PAD), rows(1), rows(1),
                  _resident((HY_PAD, HY_PAD)), _resident((1, HY_PAD)), _resident((1, HY_PAD)),
                  _resident((HY_PAD, HY_PAD)), _resident((1, HY_PAD)), _resident((1, HY_PAD)),
                  _resident((HY_PAD, HY_OUT)), _resident((1, HY_OUT))],
        out_specs=(rows(HY_OUT // 2), rows(HY_OUT // 2)),
        compiler_params=_cp("arbitrary"),
        name="hyena_filters",
    )(feats_c, tn_c, feats_a, tn_a, sign_a, _pad2(w1, HY_PAD, HY_PAD), row(b1), row(fr1), _pad2(w2, HY_PAD, HY_PAD),
      row(b2), row(fr2), _pad2(w3, HY_PAD, HY_OUT), decay.astype(f32).reshape(1, HY_OUT))


HY_N2 = 128
HY_NJ = 16


def _dft_consts(seq_len):
    n = 2 * seq_len
    n1h = seq_len // HY_N2
    k1 = np.arange(n1h)
    th = 2.0 * np.pi * np.outer(k1 + 0.5, np.arange(n1h)) / (2 * n1h)
    f_a = np.concatenate([np.cos(th), -np.sin(th)], axis=0)
    f_c = np.concatenate([np.cos(th).T, -np.sin(th).T], axis=1)
    n2 = np.arange(HY_N2)
    ph = -2.0 * np.pi * (n2[None, None, :] * (k1[:, None, None] + 0.5) / n + np.outer(n2, n2)[None] / HY_N2)
    gr, gi = np.cos(ph), np.sin(ph)
    g = np.concatenate([np.concatenate([gr, -gi], 2), np.concatenate([gi, gr], 2)], 1)
    hr, hi = np.transpose(gr, (0, 2, 1)), -np.transpose(gi, (0, 2, 1))
    h = np.concatenate([np.concatenate([hr, -hi], 2), np.concatenate([hi, hr], 2)], 1)
    return (jnp.asarray(f_a, f32), jnp.asarray(f_c, f32), jnp.asarray(g, f32), jnp.asarray(h, f32))


def _dot_p(a, b, exact):
    if exact:
        return jnp.dot(a, b, preferred_element_type=f32, precision=lax.Precision.HIGHEST)
    return _bdot(a, b)


def _hy_a_kernel(z_ref, f_ref, a_ref, *, exact):
    f = f_ref[...]
    n1h = f.shape[1]
    for j in range(HY_NJ):
        a = _dot_p(f, z_ref[0, :, j, :], exact)
        a_ref[0, 0, :, j, :] = a[:n1h]
        a_ref[0, 1, :, j, :] = a[n1h:]


def hyena_stage_a(z4, col_block, width, f_a, *, exact=False):
    nb, n1h = z4.shape[0], z4.shape[1]
    return pl.pallas_call(
        functools.partial(_hy_a_kernel, exact=exact),
        out_shape=jax.ShapeDtypeStruct((nb, 2, n1h, HY_N2, width), f32),
        grid=(nb, HY_N2 // HY_NJ),
        in_specs=[pl.BlockSpec((1, n1h, HY_NJ, width), lambda b, j: (b, 0, j, col_block)),
                  _resident((2 * n1h, n1h))],
        out_specs=pl.BlockSpec((1, 2, n1h, HY_NJ, width), lambda b, j: (b, 0, 0, j, 0)),
        compiler_params=_cp("arbitrary", "arbitrary"),
        name="hyena_stage_a",
    )(z4, f_a)


def _hy_spec_kernel(a_ref, g_ref, ks_ref, *, scale):
    k1 = pl.program_id(0)
    sgn = (1 - 2 * (k1 % 2)).astype(f32)
    g = g_ref[0]
    kc = _dot_p(g, jnp.concatenate([a_ref[0, 0, 0], a_ref[0, 1, 0]], axis=0), True)
    kg = _dot_p(g, jnp.concatenate([a_ref[1, 0, 0], a_ref[1, 1, 0]], axis=0), True)
    ks_ref[0, 0] = (kc[:HY_N2] + sgn * kg[HY_N2:]) * scale
    ks_ref[1, 0] = (kc[HY_N2:] - sgn * kg[:HY_N2]) * scale


def hyena_filter_spectrum(a_taps, g_mats, *, seq_len):
    n1h = a_taps.shape[2]
    width = a_taps.shape[-1]
    return pl.pallas_call(
        functools.partial(_hy_spec_kernel, scale=1.0 / seq_len),
        out_shape=jax.ShapeDtypeStruct((2, n1h, HY_N2, width), f32),
        grid=(n1h,),
        in_specs=[pl.BlockSpec((2, 2, 1, HY_N2, width), lambda k: (0, 0, k, 0, 0)),
                  pl.BlockSpec((1, 2 * HY_N2, 2 * HY_N2), lambda k: (k, 0, 0))],
        out_specs=pl.BlockSpec((2, 1, HY_N2, width), lambda k: (0, k, 0, 0)),
        compiler_params=_cp("arbitrary"),
        name="hyena_filter_spectrum",
    )(a_taps, g_mats)


HY_BB = 4


def _hy_b_kernel(a_ref, g_ref, h_ref, ks_ref, c_ref):
    g, h = g_ref[0], h_ref[0]
    kr, ki = ks_ref[0, 0], ks_ref[1, 0]
    for b in range(a_ref.shape[0]):
        s = _bdot(g, jnp.concatenate([a_ref[b, 0, 0], a_ref[b, 1, 0]], axis=0))
        sr, si = s[:HY_N2], s[HY_N2:]
        c = _bdot(h, jnp.concatenate([sr * kr - si * ki, sr * ki + si * kr], axis=0))
        c_ref[b, 0, 0] = c[:HY_N2]
        c_ref[b, 1, 0] = c[HY_N2:]


def hyena_stage_b(a, g_mats, h_mats, kspec, order):
    nb, _, n1h, _, width = a.shape
    bb = math.gcd(nb, HY_BB)
    return pl.pallas_call(
        _hy_b_kernel,
        out_shape=jax.ShapeDtypeStruct(a.shape, f32),
        grid=(n1h, nb // bb),
        in_specs=[pl.BlockSpec((bb, 2, 1, HY_N2, width), lambda k, b: (b, 0, k, 0, 0)),
                  pl.BlockSpec((1, 2 * HY_N2, 2 * HY_N2), lambda k, b: (k, 0, 0)),
                  pl.BlockSpec((1, 2 * HY_N2, 2 * HY_N2), lambda k, b: (k, 0, 0)),
                  pl.BlockSpec((2, 1, HY_N2, width), lambda k, b: (0, k, 0, order))],
        out_specs=pl.BlockSpec((bb, 2, 1, HY_N2, width), lambda k, b: (b, 0, k, 0, 0)),
        compiler_params=_cp("arbitrary", "arbitrary"),
        name="hyena_stage_b",
    )(a, g_mats, h_mats, kspec)


def _hy_c_kernel(c_ref, f_ref, x_ref, v_ref, bias_ref, o_ref, y_ref):
    f = f_ref[...]
    for j in range(HY_NJ):
        c = jnp.concatenate([c_ref[0, 0, :, j, :], c_ref[0, 1, :, j, :]], axis=0)
        y_ref[:, j, :] = _bdot(f, c)
    v = v_ref[0]
    o_ref[0] = x_ref[0] * (y_ref[...] + bias_ref[...].reshape(1, 1, -1) * v)


def hyena_stage_c(c, f_c, x4, x_block, v4, v_block, bias):
    nb, _, n1h, _, width = c.shape
    blk = lambda col: pl.BlockSpec((1, n1h, HY_NJ, width), lambda b, j: (b, 0, j, col))
    return pl.pallas_call(
        _hy_c_kernel,
        out_shape=jax.ShapeDtypeStruct((nb, n1h, HY_N2, width), f32),
        grid=(nb, HY_N2 // HY_NJ),
        in_specs=[pl.BlockSpec((1, 2, n1h, HY_NJ, width), lambda b, j: (b, 0, 0, j, 0)),
                  _resident((n1h, 2 * n1h)),
                  blk(x_block), blk(v_block),
                  _resident((1, width))],
        out_specs=blk(0),
        scratch_shapes=[pltpu.VMEM((n1h, HY_NJ, width), f32)],
        compiler_params=_cp("arbitrary", "arbitrary"),
        name="hyena_stage_c",
    )(c, f_c, x4, v4, bias)


def hyena_lat(s, taps, bias):
    nb, seq_len, _ = s.shape
    n1h = seq_len // HY_N2
    f_a, f_c, g_mats, h_mats = _dft_consts(seq_len)
    taps4 = jnp.stack(taps).reshape(2, n1h, HY_N2, 2 * D_W)
    kspec = hyena_filter_spectrum(hyena_stage_a(taps4, 0, 2 * D_W, f_a, exact=True), g_mats, seq_len=seq_len)
    g_bf, h_bf = g_mats.astype(bf16), h_mats.astype(bf16)
    s4 = s.reshape(nb, n1h, HY_N2, 3 * D_W)
    c0 = hyena_stage_b(hyena_stage_a(s4, 0, D_W, f_a), g_bf, h_bf, kspec, 0)
    z1 = hyena_stage_c(c0, f_c, s4, 1, s4, 0, bias[0:1])
    c1 = hyena_stage_b(hyena_stage_a(z1, 0, D_W, f_a), g_bf, h_bf, kspec, 1)
    z2 = hyena_stage_c(c1, f_c, s4, 2, z1, 0, bias[1:2])
    return z2.reshape(nb, seq_len, D_W)


def _ctx_dft_consts(seq_len):
    n = 2 * seq_len
    k = np.arange(seq_len) + 0.5
    th = 2.0 * np.pi * np.outer(k, np.arange(n)) / n
    fwd_full = np.concatenate([np.cos(th), -np.sin(th)], axis=0)
    inv = np.concatenate([np.cos(th[:, :seq_len]).T, -np.sin(th[:, :seq_len]).T], axis=1) * (2.0 / n)
    return jnp.asarray(fwd_full, f32), jnp.asarray(fwd_full[:, :seq_len], f32), jnp.asarray(inv, f32)


def _ctx_spec_kernel(f_ref, t_ref, o_ref):
    o_ref[...] = _dot_p(f_ref[...], t_ref[...], True)


def _hy_ctx_kernel(v_ref, x1_ref, x2_ref, k0_ref, k1_ref, ff_ref, fi_ref, bias_ref, o_ref):
    ff, fi = ff_ref[...], fi_ref[...]
    n = ff.shape[0] // 2
    z = v_ref[0]
    for x_ref, k_ref, o in ((x1_ref, k0_ref, 0), (x2_ref, k1_ref, 1)):
        s = _bdot(ff, z)
        sr, si = s[:n], s[n:]
        kr, ki = k_ref[:n], k_ref[n:]
        y = _bdot(fi, jnp.concatenate([sr * kr - si * ki, sr * ki + si * kr], axis=0))
        z = x_ref[0] * (y + bias_ref[o:o + 1] * z)
    o_ref[0] = z


HY_CTX_CB = 256


def hyena_ctx(s, taps, bias):
    nb, seq_len, _ = s.shape
    n = 2 * seq_len
    fwd_full, fwd, inv = _ctx_dft_consts(seq_len)
    taps = jnp.concatenate(taps, axis=0)
    kspec = pl.pallas_call(
        _ctx_spec_kernel,
        out_shape=jax.ShapeDtypeStruct((n, 2 * D_W), f32),
        name="hyena_ctx_spectrum",
    )(fwd_full, taps)
    ncb = D_W // HY_CTX_CB
    col = lambda off: pl.BlockSpec((1, seq_len, HY_CTX_CB), lambda b, j: (b, 0, off + j))
    return pl.pallas_call(
        _hy_ctx_kernel,
        out_shape=jax.ShapeDtypeStruct((nb, seq_len, D_W), f32),
        grid=(nb, ncb),
        in_specs=[col(0), col(ncb), col(2 * ncb),
                  pl.BlockSpec((n, HY_CTX_CB), lambda b, j: (0, j)),
                  pl.BlockSpec((n, HY_CTX_CB), lambda b, j: (0, ncb + j)),
                  _resident((n, seq_len)), _resident((seq_len, n)),
                  pl.BlockSpec((2, HY_CTX_CB), lambda b, j: (0, j))],
        out_specs=pl.BlockSpec((1, seq_len, HY_CTX_CB), lambda b, j: (b, 0, j)),
        compiler_params=_cp("arbitrary", "arbitrary"),
        name="hyena_ctx",
    )(s, s, s, kspec, kspec, fwd.astype(bf16), inv.astype(bf16), bias)


ROW_TILE = 512
ODD_ROWS = 512


def _even_layer(l, xc, xs, mod_c, mod_s, dims, caches, g_pre, g_post, w_in, w_out, rpb, lam_rows, gs, rope_t):
    (nbc, lc), (nbs, ls) = dims
    e = l // 2
    lam_init = 0.8 - 0.6 * math.exp(-0.3 * l)
    lamv = jnp.zeros((8, LANES), f32).at[:4, :DQK].set(lam_rows)
    w_in, w_out = w_in.astype(bf16), w_out.astype(bf16)
    cak, cav, cbk, cbv = caches
    cos_t, sin_t = rope_t
    pc = even_proj(xc, mod_c, g_pre, w_in, cos_t[:ROW_TILE], sin_t[:ROW_TILE], seq_len=lc, rope=False,
                   out_dtype=f32, tm=ROW_TILE)
    ps = even_proj(xs, mod_s, g_pre, w_in, cos_t, sin_t, seq_len=ls, rope=True, out_dtype=bf16, tm=ROW_TILE)
    oa_c, ob_c = ctx_even_attn(pc, lamv, gs, lam_init=lam_init, n_seq=nbc, seq_len=lc)
    tl, tr = na_bias_tables(rpb)
    oa_s = lat_na_attn(ps, cak, cav, tl, tr, e=e, n_seq=nbs, seq_len=ls)
    ob_s = lat_diff_attn(ps, cbk, cbv, lamv, gs, e=e, lam_init=lam_init, n_seq=nbs, seq_len=ls)
    xc = even_out(xc, oa_c, ob_c, mod_c, g_post, w_out, seq_len=lc, tm=ROW_TILE)
    xs = even_out(xs, oa_s, ob_s, mod_s, g_post, w_out, seq_len=ls, tm=ROW_TILE)

    def unblock(lo):
        return jnp.transpose(pc[lo:lo + 4], (1, 0, 2)).reshape(nbc, lc, D // 2)

    new_cache = (unblock(4).reshape(nbc, lc, HA, DH), unblock(8).reshape(nbc, lc, HA, DH),
                 unblock(16).reshape(nbc, lc, HB, 2 * DQK), unblock(20).reshape(nbc, lc, HB, DVB))
    return xc, xs, new_cache


def _odd_mixer(x, mod, h0, g_pre, g_post, w_in, w_out, s5p, glu, hyp):
    nb, seq_len, _ = x.shape
    tt = ODD_ROWS // nb
    u, hy = odd_proj(x, mod, g_pre, w_in, tt=tt)
    conv_w, conv_b, filt, bias = hyp
    s = short_conv(hy, conv_w, conv_b)
    taps = hyena_filters(seq_len, *filt)
    yd = hyena_lat(s, taps, bias) if seq_len > 2 * HY_N2 else hyena_ctx(s, taps, bias)
    w_glu, b_glu, d_skip = glu
    y = d_skip
    fins = []
    u2 = u.reshape(seq_len * nb, C_W)
    for dr in range(2):
        a, bm, cm = s5p[dr]
        y, fin = s5_direction(u2, y, h0[dr], a, bm, cm, w_glu, b_glu, nb=nb, tc=tt, reverse=(dr == 1))
        fins.append(fin)
    x = odd_out(x, y.reshape(seq_len, nb, C_W), yd, mod, g_post, w_out, tt=tt)
    return x, fins


def kernel(x_prompt, x_sample, cache_a_k, cache_a_v, cache_b_k, cache_b_v, state_c_re, state_c_im, c, c_ctx, w_mod, b_mod, g_mix_pre, g_mix_post, g_ffn_pre, g_ffn_post, w_ffn_gate, w_ffn_up, w_ffn_down, w_in_e, w_out_e, na_rpb, lam_q1, lam_k1, lam_q2, lam_k2, g_subln, w_in_o, w_out_o, ssm_lam_re, ssm_lam_im, ssm_log_dt, ssm_b_re, ssm_b_im, ssm_c_re, ssm_c_im, ssm_d, w_glu, b_glu, hy_conv_w, hy_conv_b, hy_w1, hy_b1, hy_fr1, hy_w2, hy_b2, hy_fr2, hy_w3, hy_decay, hy_bias):
    nbc, lc, _ = x_prompt.shape
    nbs, ls, _ = x_sample.shape
    n_even, past = cache_a_k.shape[1], cache_a_k.shape[2]
    dims = ((nbc, lc), (nbs, ls))
    cond = jnp.concatenate([c_ctx[None], c, jnp.zeros((16 - 1 - nbs, D), f32)], axis=0)
    mod_all = modulation(cond, w_mod, b_mod)
    xc = x_prompt.reshape(nbc * lc, D)
    xs = x_sample.reshape(nbs * ls, D)
    caches = tuple(t.reshape(nbs, n_even, past, D // 2) for t in (cache_a_k, cache_a_v, cache_b_k, cache_b_v))
    rope_t = rope_tables(ls)
    new_caches, new_sre, new_sim = [], [], []
    row = lambda v: v[None]
    for l in range(DEPTH):
        mod_c = mod_all[l, 0:1].reshape(1, 1, 6 * D)
        mod_s = mod_all[l, 1:1 + nbs].reshape(nbs, 1, 6 * D)
        if l % 2 == 0:
            e = l // 2
            lam_rows = jnp.stack([lam_q1[e], lam_k1[e], lam_q2[e], lam_k2[e]])
            xc, xs, nc = _even_layer(l, xc, xs, mod_c, mod_s, dims, caches, row(g_mix_pre[l]), row(g_mix_post[l]),
                                     w_in_e[e], w_out_e[e], na_rpb[e], lam_rows, row(g_subln[e]), rope_t)
            new_caches.append(nc)
        else:
            o = l // 2
            s5p = [s5_prepare(ssm_lam_re[o, dr], ssm_lam_im[o, dr], ssm_log_dt[o, dr], ssm_b_re[o, dr], ssm_b_im[o, dr],
                              ssm_c_re[o, dr], ssm_c_im[o, dr]) for dr in range(2)]
            glu = (w_glu[o].astype(bf16), row(b_glu[o]), row(ssm_d[o]))
            filt = (hy_w1[o], hy_b1[o], hy_fr1[o], hy_w2[o], hy_b2[o], hy_fr2[o], hy_w3[o], hy_decay[o])
            hyp = (hy_conv_w[o], row(hy_conv_b[o]), filt, hy_bias[o])
            w_in, w_out = w_in_o[o].astype(bf16), w_out_o[o].astype(bf16)
            zero_state = jnp.zeros((nbc, 2 * S5_STATES), f32)
            xc3, fins = _odd_mixer(xc.reshape(nbc, lc, D), jnp.broadcast_to(mod_c, (nbc, 1, 6 * D)), (zero_state, zero_state),
                                   row(g_mix_pre[l]), row(g_mix_post[l]), w_in, w_out, s5p, glu, hyp)
            h0 = [jnp.concatenate([state_c_re[:, o, dr].reshape(nbs, S5_STATES), state_c_im[:, o, dr].reshape(nbs, S5_STATES)],
                                  axis=1) for dr in range(2)]
            xs3, _ = _odd_mixer(xs.reshape(nbs, ls, D), mod_s, h0, row(g_mix_pre[l]), row(g_mix_post[l]), w_in, w_out,
                                s5p, glu, hyp)
            xc, xs = xc3.reshape(nbc * lc, D), xs3.reshape(nbs * ls, D)
            new_sre.append(jnp.stack([f[:, :S5_STATES].reshape(nbc, C_G, C_P) for f in fins], axis=1))
            new_sim.append(jnp.stack([f[:, S5_STATES:].reshape(nbc, C_G, C_P) for f in fins], axis=1))
        wg, wu, wd = w_ffn_gate[l].astype(bf16), w_ffn_up[l].astype(bf16), w_ffn_down[l].astype(bf16)
        xc = ffn(xc, mod_c, row(g_ffn_pre[l]), row(g_ffn_post[l]), wg, wu, wd, seq_len=lc, tm=ROW_TILE)
        xs = ffn(xs, mod_s, row(g_ffn_pre[l]), row(g_ffn_post[l]), wg, wu, wd, seq_len=ls, tm=ROW_TILE)
    new_a_k, new_a_v, new_b_k, new_b_v = (jnp.stack([nc[i] for nc in new_caches], axis=1) for i in range(4))
    return (xc.reshape(nbc, lc, D), xs.reshape(nbs, ls, D), new_a_k, new_a_v, new_b_k, new_b_v,
            jnp.stack(new_sre, axis=1), jnp.stack(new_sim, axis=1))
```

```python
import functools
import math

import numpy as np
import jax
import jax.numpy as jnp
from jax import lax
from jax.experimental import pallas as pl
from jax.experimental.pallas import tpu as pltpu

f32 = jnp.float32
bf16 = jnp.bfloat16

D = 1024
DEPTH = 4
D_FF = 2816
GRID_W = 64
HA, DH = 8, 64
HB, DQK, DVB = 4, 64, 128
NA_KH, NA_KW = 8, 16
C_W, C_G, C_GH, C_P = 512, 32, 16, 64
D_W = 512
HY_EMB_BANDS = 16
HY_BAND_MIN = 1e-4
ROPE_BASE = 10000.0
EPS = 1e-6
DT_MIN = 1e-3
DT_MAX = 1e-1
NEG = -1e30
LOG2E = math.log2(math.e)
Q_PRESCALE = DH ** -0.5 * LOG2E

LANES = 128
VMEM_LIMIT = 56 * 1024 * 1024


def _cp(*sem):
    return pltpu.CompilerParams(dimension_semantics=sem, vmem_limit_bytes=VMEM_LIMIT)


def _bdot(a, b):
    return jnp.dot(a.astype(bf16), b.astype(bf16), preferred_element_type=f32)


def _bdot_t(a, b):
    return lax.dot_general(a.astype(bf16), b.astype(bf16), (((1,), (1,)), ((), ())),
                           preferred_element_type=f32)


def _rms(x, g):
    return x * lax.rsqrt(jnp.mean(x * x, axis=-1, keepdims=True) + EPS) * g


def _silu(x):
    return x * (1.0 / (1.0 + jnp.exp(-x)))


def _resident(shape):
    nd = len(shape)
    return pl.BlockSpec(shape, lambda *_: (0,) * nd, pipeline_mode=pl.Buffered(1))


def _mod_kernel(c_ref, w_ref, b_ref, o_ref):
    o_ref[0] = _bdot(_silu(c_ref[...]), w_ref[0]) + b_ref[0]


def modulation(cond, w_mod, b_mod):
    nb = 4
    cb = 6 * D // nb
    return pl.pallas_call(
        _mod_kernel,
        out_shape=jax.ShapeDtypeStruct((DEPTH, 16, 6 * D), f32),
        grid=(DEPTH, nb),
        in_specs=[pl.BlockSpec((16, D), lambda l, j: (0, 0)),
                  pl.BlockSpec((1, D, cb), lambda l, j: (l, 0, j)),
                  pl.BlockSpec((1, 1, cb), lambda l, j: (l, 0, j))],
        out_specs=pl.BlockSpec((1, 16, cb), lambda l, j: (l, 0, j)),
        compiler_params=_cp("arbitrary", "arbitrary"),
        name="modulation",
    )(cond, w_mod, b_mod.reshape(DEPTH, 1, 6 * D))


def _mod_slice(m_ref, k):
    return m_ref[0, :, k * D:(k + 1) * D]


def _mod_map(n_seq, tiles_per_seq):
    if n_seq == 1:
        return lambda i: (0, 0, 0)
    return lambda i: (i // tiles_per_seq, 0, 0)


def _even_proj_kernel(x_ref, m_ref, g_ref, w_ref, cs_ref, sn_ref, o_ref, *, rope):
    h = _rms(x_ref[...], g_ref[...]) * (1.0 + _mod_slice(m_ref, 1)) + _mod_slice(m_ref, 0)
    p = _bdot(h, w_ref[...])
    if rope:
        lane = lax.broadcasted_iota(jnp.int32, (1, LANES), 1)
        first = (lane % 32) < 16
        c, s = cs_ref[...], sn_ref[...]
    for j in range(24):
        blk = p[:, j * LANES:(j + 1) * LANES]
        if rope and 12 <= j < 20:
            partner = jnp.where(first, pltpu.roll(blk, LANES - 16, axis=1), pltpu.roll(blk, 16, axis=1))
            blk = blk * c + partner * s
        if j < 4 or 12 <= j < 16:
            blk = blk * Q_PRESCALE
        o_ref[j] = blk.astype(o_ref.dtype)


def even_proj(x, mod, g, w, cos_t, sin_t, *, seq_len, rope, out_dtype, tm):
    rows = x.shape[0]
    n_seq = mod.shape[0]
    tps = seq_len // tm if n_seq > 1 else 1
    tbl_map = (lambda i: (i % tps, 0)) if rope else (lambda i: (0, 0))
    return pl.pallas_call(
        functools.partial(_even_proj_kernel, rope=rope),
        out_shape=jax.ShapeDtypeStruct((24, rows, LANES), out_dtype),
        grid=(rows // tm,),
        in_specs=[pl.BlockSpec((tm, D), lambda i: (i, 0)),
                  pl.BlockSpec((1, 1, 6 * D), _mod_map(n_seq, tps)),
                  _resident((1, D)),
                  _resident((D, 3 * D)),
                  pl.BlockSpec((tm, LANES), tbl_map),
                  pl.BlockSpec((tm, LANES), tbl_map)],
        out_specs=pl.BlockSpec((24, tm, LANES), lambda i: (0, i, 0)),
        compiler_params=_cp("arbitrary"),
        name="even_proj",
    )(x, mod, g, w, cos_t, sin_t)


def _even_out_kernel(x_ref, oa_ref, ob_ref, m_ref, g_ref, w_ref, o_ref):
    w = w_ref[...]
    o = _bdot(oa_ref[...], w[:D // 2]) + _bdot(ob_ref[...], w[D // 2:])
    o_ref[...] = x_ref[...] + _mod_slice(m_ref, 2) * _rms(o, g_ref[...])


def even_out(x, oa, ob, mod, g, w, *, seq_len, tm):
    rows = x.shape[0]
    n_seq = mod.shape[0]
    tps = seq_len // tm if n_seq > 1 else 1
    return pl.pallas_call(
        _even_out_kernel,
        out_shape=jax.ShapeDtypeStruct((rows, D), f32),
        grid=(rows // tm,),
        in_specs=[pl.BlockSpec((tm, D), lambda i: (i, 0)),
                  pl.BlockSpec((tm, D // 2), lambda i: (i, 0)),
                  pl.BlockSpec((tm, D // 2), lambda i: (i, 0)),
                  pl.BlockSpec((1, 1, 6 * D), _mod_map(n_seq, tps)),
                  _resident((1, D)),
                  _resident((D, D))],
        out_specs=pl.BlockSpec((tm, D), lambda i: (i, 0)),
        compiler_params=_cp("arbitrary"),
        name="even_out",
    )(x, oa, ob, mod, g, w)


FF_CHUNK = D_FF // 2


def _ffn_kernel(x_ref, m_ref, gpre_ref, gpost_ref, wg_ref, wu_ref, wd_ref, o_ref):
    x = x_ref[...]
    h = (_rms(x, gpre_ref[...]) * (1.0 + _mod_slice(m_ref, 4)) + _mod_slice(m_ref, 3)).astype(bf16)
    acc = None
    for c in range(D_FF // FF_CHUNK):
        sl = slice(c * FF_CHUNK, (c + 1) * FF_CHUNK)
        a = _silu(_bdot(h, wg_ref[:, sl])) * _bdot(h, wu_ref[:, sl])
        part = _bdot(a, wd_ref[sl, :])
        acc = part if acc is None else acc + part
    o_ref[...] = x + _mod_slice(m_ref, 5) * _rms(acc, gpost_ref[...])


def ffn(x, mod, gpre, gpost, wg, wu, wd, *, seq_len, tm):
    rows = x.shape[0]
    n_seq = mod.shape[0]
    tps = seq_len // tm if n_seq > 1 else 1
    return pl.pallas_call(
        _ffn_kernel,
        out_shape=jax.ShapeDtypeStruct((rows, D), f32),
        grid=(rows // tm,),
        in_specs=[pl.BlockSpec((tm, D), lambda i: (i, 0)),
                  pl.BlockSpec((1, 1, 6 * D), _mod_map(n_seq, tps)),
                  _resident((1, D)),
                  _resident((1, D)),
                  _resident((D, D_FF)),
                  _resident((D, D_FF)),
                  _resident((D_FF, D))],
        out_specs=pl.BlockSpec((tm, D), lambda i: (i, 0)),
        compiler_params=_cp("arbitrary"),
        name="ffn",
    )(x, mod, gpre, gpost, wg, wu, wd)


def rope_tables(seq_len):
    lane = np.arange(LANES)
    d = lane % DQK
    nf = DQK // 4
    inv = ROPE_BASE ** (-jnp.arange(nf, dtype=f32) / nf)
    t = jnp.arange(seq_len)
    pos = jnp.where((d < DQK // 2)[None, :], (t // GRID_W)[:, None], (t % GRID_W)[:, None]).astype(f32)
    ang = pos * inv[d % nf][None, :]
    sign = np.where((d % (DQK // 2)) < nf, -1.0, 1.0).astype(np.float32)
    return jnp.cos(ang), jnp.sin(ang) * sign[None, :]


def _diff_lambda(lam_ref, lam_init):
    l = lam_ref[...]
    a1 = jnp.sum(l[0:1] * l[1:2], axis=1, keepdims=True)
    a2 = jnp.sum(l[2:3] * l[3:4], axis=1, keepdims=True)
    return jnp.exp(a1) - jnp.exp(a2) + lam_init


def _half_masks():
    lane = lax.broadcasted_iota(jnp.int32, (1, LANES), 1)
    lo = lane < (LANES // 2)
    return lo, jnp.logical_not(lo)


def _softmax_parts(s):
    m = jnp.max(s, axis=-1, keepdims=True)
    e = jnp.exp2(s - m)
    return e, jnp.sum(e, axis=-1, keepdims=True)


def _ctx_attn_kernel(p_ref, lam_ref, gs_ref, oa_ref, ob_ref, *, lam_init):
    lo, hi = _half_masks()
    lam = _diff_lambda(lam_ref, lam_init)
    for pr in range(HA // 2):
        q, k, v = p_ref[pr], p_ref[4 + pr], p_ref[8 + pr]
        outs = []
        for msk in (lo, hi):
            e, l = _softmax_parts(_bdot_t(jnp.where(msk, q, 0.0), k))
            outs.append(_bdot(e, v) / l)
        oa_ref[:, pr * LANES:(pr + 1) * LANES] = jnp.where(lo, outs[0], outs[1])
    for h in range(HB):
        q, k, v = p_ref[12 + h], p_ref[16 + h], p_ref[20 + h]
        ps = []
        for msk in (lo, hi):
            e, l = _softmax_parts(_bdot_t(jnp.where(msk, q, 0.0), k))
            ps.append(e / l)
        o = _bdot(ps[0] - lam * ps[1], v)
        ob_ref[:, h * LANES:(h + 1) * LANES] = _rms(o, gs_ref[...]) * (1.0 - lam_init)


def ctx_even_attn(p, lamv, gs, *, lam_init, n_seq, seq_len):
    rows = n_seq * seq_len
    return pl.pallas_call(
        functools.partial(_ctx_attn_kernel, lam_init=lam_init),
        out_shape=(jax.ShapeDtypeStruct((rows, D // 2), f32), jax.ShapeDtypeStruct((rows, D // 2), f32)),
        grid=(n_seq,),
        in_specs=[pl.BlockSpec((24, seq_len, LANES), lambda b: (0, b, 0)),
                  _resident((8, LANES)),
                  _resident((1, DVB))],
        out_specs=(pl.BlockSpec((seq_len, D // 2), lambda b: (b, 0)),
                   pl.BlockSpec((seq_len, D // 2), lambda b: (b, 0))),
        compiler_params=_cp("arbitrary"),
        name="ctx_even_attn",
    )(p, lamv, gs)


NA_QROWS = 4
NA_KROWS = NA_QROWS + NA_KH
NA_TBL = 2 * NA_KH
NA_SUB = 4


def na_bias_tables(rpb):
    cols = np.arange(GRID_W)
    cs = np.clip(cols - NA_KW // 2, 0, GRID_W - NA_KW)
    cp = cols[None, :]
    inwin = (cp >= cs[:, None]) & (cp < cs[:, None] + NA_KW)
    dcol = np.clip(cp - cols[:, None] + (NA_KW - 1), 0, 2 * NA_KW - 2)
    tm = jnp.where(inwin[None, None], rpb[:, :, dcol] * LOG2E, NEG)
    tm = jnp.concatenate([tm, jnp.full((HA, 1, GRID_W, GRID_W), NEG, f32)], axis=1)
    z = jnp.zeros_like(tm)
    return jnp.concatenate([tm, z], axis=-1), jnp.concatenate([z, tm], axis=-1)


def _na_kernel(q_ref, k_ref, v_ref, kc_ref, vc_ref, tl_ref, tr_ref, o_ref, *, n_rows):
    kc = kc_ref[0, 0].astype(bf16)
    vc = vc_ref[0, 0].astype(bf16)
    lo, hi = _half_masks()
    tq = NA_QROWS * GRID_W
    for sub in range(NA_SUB):
        i = pl.program_id(2) * NA_SUB + sub
        ws = jnp.clip(NA_QROWS * i - NA_KH // 2, 0, n_rows - NA_KROWS)
        start = pl.multiple_of(ws * GRID_W, GRID_W)
        q = q_ref[0, sub * tq:(sub + 1) * tq, :]
        kl = k_ref[0, pl.ds(start, NA_KROWS * GRID_W), :]
        vl = v_ref[0, pl.ds(start, NA_KROWS * GRID_W), :]
        outs = []
        for hh, msk in enumerate((lo, hi)):
            qm = jnp.where(msk, q, jnp.zeros_like(q))
            s_loc = _bdot_t(qm, kl)
            s_ctx = _bdot_t(qm, kc)
            bias_rows = []
            for qr in range(NA_QROWS):
                r = NA_QROWS * i + qr
                rs = jnp.clip(r - NA_KH // 2, 0, n_rows - NA_KH)

                def tbl_idx(kr):
                    rp = ws + kr
                    valid = jnp.logical_and(rp >= rs, rp < rs + NA_KH)
                    return jnp.where(valid, rp - r + (NA_KH - 1), NA_TBL - 1)

                tiles = [tl_ref[hh, tbl_idx(2 * kp)] + tr_ref[hh, tbl_idx(2 * kp + 1)]
                         for kp in range(NA_KROWS // 2)]
                bias_rows.append(jnp.concatenate(tiles, axis=1))
            s_loc = s_loc + jnp.concatenate(bias_rows, axis=0)
            m = jnp.maximum(jnp.max(s_loc, axis=-1, keepdims=True), jnp.max(s_ctx, axis=-1, keepdims=True))
            e_loc = jnp.exp2(s_loc - m)
            e_ctx = jnp.exp2(s_ctx - m)
            l = jnp.sum(e_loc, axis=-1, keepdims=True) + jnp.sum(e_ctx, axis=-1, keepdims=True)
            outs.append((_bdot(e_loc, vl) + _bdot(e_ctx, vc)) / l)
        o_ref[sub * tq:(sub + 1) * tq, :] = jnp.where(lo, outs[0], outs[1])


def lat_na_attn(p, cache_k, cache_v, tl, tr, *, e, n_seq, seq_len):
    n_rows = seq_len // GRID_W
    steps = n_rows // (NA_QROWS * NA_SUB)
    tq = NA_QROWS * NA_SUB * GRID_W
    past = cache_k.shape[2]
    return pl.pallas_call(
        functools.partial(_na_kernel, n_rows=n_rows),
        out_shape=jax.ShapeDtypeStruct((n_seq * seq_len, D // 2), f32),
        grid=(n_seq, HA // 2, steps),
        in_specs=[pl.BlockSpec((1, tq, LANES), lambda b, pr, i: (pr, b * steps + i, 0)),
                  pl.BlockSpec((1, seq_len, LANES), lambda b, pr, i: (4 + pr, b, 0)),
                  pl.BlockSpec((1, seq_len, LANES), lambda b, pr, i: (8 + pr, b, 0)),
                  pl.BlockSpec((1, 1, past, LANES), lambda b, pr, i: (b, e, 0, pr)),
                  pl.BlockSpec((1, 1, past, LANES), lambda b, pr, i: (b, e, 0, pr)),
                  pl.BlockSpec((2, NA_TBL, GRID_W, LANES), lambda b, pr, i: (pr, 0, 0, 0)),
                  pl.BlockSpec((2, NA_TBL, GRID_W, LANES), lambda b, pr, i: (pr, 0, 0, 0))],
        out_specs=pl.BlockSpec((tq, LANES), lambda b, pr, i: (b * steps + i, pr)),
        compiler_params=_cp("arbitrary", "arbitrary", "arbitrary"),
        name="lat_na_attn",
    )(p, p, p, cache_k, cache_v, tl, tr)


DIFF_TQ = 256
DIFF_KC = 1024


def _diff_kernel(q_ref, k_ref, v_ref, kc_ref, vc_ref, lam_ref, gs_ref, o_ref, s_ref, e_ref, *, lam_init, seq_len):
    lo, hi = _half_masks()
    q = q_ref[0]
    tq = q.shape[0]
    past = kc_ref.shape[2]
    kc = kc_ref[0, 0].astype(bf16)
    chunks = [(c * DIFF_KC, DIFF_KC) for c in range(seq_len // DIFF_KC)] + [(seq_len, past)]
    outs = []
    for mi, msk in enumerate((lo, hi)):
        qm = jnp.where(msk, q, jnp.zeros_like(q))
        mpart = jnp.full((tq, LANES), NEG, f32)
        for off, n in chunks:
            s = _bdot_t(qm, kc if off == seq_len else k_ref[0, off:off + n, :])
            s_ref[mi, :, off:off + n] = s
            for j in range(n // LANES):
                mpart = jnp.maximum(mpart, s[:, j * LANES:(j + 1) * LANES])
        m = jnp.max(mpart, axis=-1, keepdims=True)
        lpart = jnp.zeros((tq, LANES), f32)
        for off, n in chunks:
            e = jnp.exp2(s_ref[mi, :, off:off + n] - m)
            for j in range(n // LANES):
                lpart = lpart + e[:, j * LANES:(j + 1) * LANES]
            e_ref[mi, :, off:off + n] = e.astype(bf16)
        l = jnp.sum(lpart, axis=-1, keepdims=True)
        acc = _bdot(e_ref[mi, :, :seq_len], v_ref[0]) + _bdot(e_ref[mi, :, seq_len:], vc_ref[0, 0].astype(bf16))
        outs.append(acc / l)
    o = outs[0] - _diff_lambda(lam_ref, lam_init) * outs[1]
    o_ref[...] = _rms(o, gs_ref[...]) * (1.0 - lam_init)


def lat_diff_attn(p, cache_k, cache_v, lamv, gs, *, e, lam_init, n_seq, seq_len):
    steps = seq_len // DIFF_TQ
    past = cache_k.shape[2]
    return pl.pallas_call(
        functools.partial(_diff_kernel, lam_init=lam_init, seq_len=seq_len),
        out_shape=jax.ShapeDtypeStruct((n_seq * seq_len, D // 2), f32),
        grid=(n_seq, HB, steps),
        scratch_shapes=[pltpu.VMEM((2, DIFF_TQ, seq_len + past), f32), pltpu.VMEM((2, DIFF_TQ, seq_len + past), bf16)],
        in_specs=[pl.BlockSpec((1, DIFF_TQ, LANES), lambda b, h, i: (12 + h, b * steps + i, 0)),
                  pl.BlockSpec((1, seq_len, LANES), lambda b, h, i: (16 + h, b, 0)),
                  pl.BlockSpec((1, seq_len, LANES), lambda b, h, i: (20 + h, b, 0)),
                  pl.BlockSpec((1, 1, past, LANES), lambda b, h, i: (b, e, 0, h)),
                  pl.BlockSpec((1, 1, past, LANES), lambda b, h, i: (b, e, 0, h)),
                  _resident((8, LANES)),
                  _resident((1, DVB))],
        out_specs=pl.BlockSpec((DIFF_TQ, LANES), lambda b, h, i: (b * steps + i, h)),
        compiler_params=_cp("arbitrary", "arbitrary", "arbitrary"),
        name="lat_diff_attn",
    )(p, p, p, cache_k, cache_v, lamv, gs)


def _odd_proj_kernel(x_ref, m_ref, g_ref, w_ref, u_ref, hy_ref, *, nb):
    m = m_ref[...]
    tt = x_ref.shape[1]
    h = _rms(x_ref[...], g_ref[...]) * (1.0 + m[:, :, D:2 * D]) + m[:, :, :D]
    p = _bdot(h.reshape(nb * tt, D), w_ref[...]).reshape(nb, tt, C_W + 3 * D_W)
    hy_ref[...] = p[:, :, C_W:]
    for b in range(nb):
        u_ref[:, b, :] = p[b, :, :C_W]


def odd_proj(x, mod, g, w, *, tt):
    nb, seq_len, _ = x.shape
    return pl.pallas_call(
        functools.partial(_odd_proj_kernel, nb=nb),
        out_shape=(jax.ShapeDtypeStruct((seq_len, nb, C_W), f32),
                   jax.ShapeDtypeStruct((nb, seq_len, 3 * D_W), f32)),
        grid=(seq_len // tt,),
        in_specs=[pl.BlockSpec((nb, tt, D), lambda i: (0, i, 0)),
                  _resident((nb, 1, 6 * D)),
                  _resident((1, D)),
                  _resident((D, C_W + 3 * D_W))],
        out_specs=(pl.BlockSpec((tt, nb, C_W), lambda i: (i, 0, 0)),
                   pl.BlockSpec((nb, tt, 3 * D_W), lambda i: (0, i, 0))),
        compiler_params=_cp("arbitrary"),
        name="odd_proj",
    )(x, mod, g, w)


def _odd_out_kernel(x_ref, yc_ref, yd_ref, m_ref, g_ref, w_ref, o_ref, *, nb):
    tt = x_ref.shape[1]
    yc = jnp.concatenate([yc_ref[:, b, :] for b in range(nb)], axis=0)
    o = _bdot(yc, w_ref[:C_W]) + _bdot(yd_ref[...].reshape(nb * tt, D_W), w_ref[C_W:])
    o_ref[...] = x_ref[...] + m_ref[...][:, :, 2 * D:3 * D] * _rms(o, g_ref[...]).reshape(nb, tt, D)


def odd_out(x, yc, yd, mod, g, w, *, tt):
    nb, seq_len, _ = x.shape
    return pl.pallas_call(
        functools.partial(_odd_out_kernel, nb=nb),
        out_shape=jax.ShapeDtypeStruct((nb, seq_len, D), f32),
        grid=(seq_len // tt,),
        in_specs=[pl.BlockSpec((nb, tt, D), lambda i: (0, i, 0)),
                  pl.BlockSpec((tt, nb, C_W), lambda i: (i, 0, 0)),
                  pl.BlockSpec((nb, tt, D_W), lambda i: (0, i, 0)),
                  _resident((nb, 1, 6 * D)),
                  _resident((1, D)),
                  _resident((D, D))],
        out_specs=pl.BlockSpec((nb, tt, D), lambda i: (0, i, 0)),
        compiler_params=_cp("arbitrary"),
        name="odd_out",
    )(x, yc, yd, mod, g, w)


S5_STATES = C_G * C_P
S5_SB = 512


def s5_prepare(lam_re, lam_im, log_dt, b_re, b_im, c_re, c_im):
    lr, li = lam_re.astype(f32), lam_im.astype(f32)
    dt = jnp.exp(log_dt.astype(f32))[:, None]
    mag = jnp.exp(lr * dt)
    ab_re, ab_im = mag * jnp.cos(li * dt), mag * jnp.sin(li * dt)
    den = lr * lr + li * li
    k_re = ((ab_re - 1.0) * lr + ab_im * li) / den
    k_im = (ab_im * lr - (ab_re - 1.0) * li) / den
    br, bi = b_re.astype(f32), b_im.astype(f32)
    bb_re = k_re[..., None] * br - k_im[..., None] * bi
    bb_im = k_re[..., None] * bi + k_im[..., None] * br
    eye = jnp.eye(C_G // 2, dtype=f32)

    def in_blocks(bb):
        return jnp.einsum('agph,gk->aghkp', bb.reshape(2, C_G // 2, C_P, C_GH), eye).reshape(2, C_W // 2, S5_STATES // 2)

    def out_blocks(c):
        return jnp.einsum('aghp,gk->agpkh', c.reshape(2, C_G // 2, C_GH, C_P), eye).reshape(2, S5_STATES // 2, C_W // 2)

    bm = jnp.concatenate([in_blocks(bb_re), in_blocks(bb_im)], axis=-1).astype(bf16)
    cm = jnp.concatenate([out_blocks(c_re.astype(f32)), -out_blocks(c_im.astype(f32))], axis=1).astype(bf16)
    a = jnp.concatenate([ab_re.reshape(1, S5_STATES), ab_im.reshape(1, S5_STATES)], axis=1)
    return a, bm, cm


def _gelu_tanh(x):
    return 0.5 * x * (1.0 + jnp.tanh(math.sqrt(2.0 / math.pi) * (x + 0.044715 * (x * x * x))))


def _s5_kernel(u_ref, add_ref, h0_ref, a_ref, bm_ref, cm_ref, wg_ref, bg_ref, y_ref, fin_ref, xs_ref, car_ref,
               *, nb, tc, reverse):
    i = pl.program_id(0)
    half_s = S5_STATES // 2

    @pl.when(i == 0)
    def _():
        car_ref[...] = h0_ref[...]

    u = u_ref[...]
    for hf in range(2):
        r = _bdot(u[:, hf * (C_W // 2):(hf + 1) * (C_W // 2)], bm_ref[hf])
        xs_ref[:, hf * half_s:(hf + 1) * half_s] = r[:, :half_s]
        xs_ref[:, S5_STATES + hf * half_s:S5_STATES + (hf + 1) * half_s] = r[:, half_s:]

    for sb in range(S5_STATES // S5_SB):
        cre = slice(sb * S5_SB, (sb + 1) * S5_SB)
        cim = slice(S5_STATES + sb * S5_SB, S5_STATES + (sb + 1) * S5_SB)
        ar = jnp.broadcast_to(a_ref[:, cre], (nb, S5_SB))
        ai = jnp.broadcast_to(a_ref[:, cim], (nb, S5_SB))

        def body(s, carry):
            xr, xi = carry
            t = (tc - 1 - s) if reverse else s
            r0 = pl.multiple_of(t * nb, nb)
            nr = ar * xr - ai * xi + xs_ref[pl.ds(r0, nb), cre]
            ni = ar * xi + ai * xr + xs_ref[pl.ds(r0, nb), cim]
            xs_ref[pl.ds(r0, nb), cre] = nr
            xs_ref[pl.ds(r0, nb), cim] = ni
            return nr, ni

        xr, xi = lax.fori_loop(0, tc, body, (car_ref[:, cre], car_ref[:, cim]), unroll=4)
        car_ref[:, cre] = xr
        car_ref[:, cim] = xi

    ys = []
    for hf in range(2):
        xre = xs_ref[:, hf * half_s:(hf + 1) * half_s]
        xim = xs_ref[:, S5_STATES + hf * half_s:S5_STATES + (hf + 1) * half_s]
        ys.append(_bdot(xre, cm_ref[hf, :half_s]) + _bdot(xim, cm_ref[hf, half_s:]))
    y = jnp.concatenate(ys, axis=1)
    if reverse:
        y = y + add_ref[...]
        g = _gelu_tanh(y)
        y = g * (1.0 / (1.0 + jnp.exp(-(_bdot(g, wg_ref[...]) + bg_ref[...]))))
    else:
        y = y + add_ref[...] * u
    y_ref[...] = y
    fin_ref[...] = car_ref[...]


def s5_direction(u, add, h0, a, bm, cm, w_glu, b_glu, *, nb, tc, reverse):
    rows = u.shape[0]
    rb = tc * nb
    nc = rows // rb
    cmap = (lambda i: (nc - 1 - i, 0)) if reverse else (lambda i: (i, 0))
    add_spec = pl.BlockSpec((rb, C_W), cmap) if reverse else _resident((1, C_W))
    return pl.pallas_call(
        functools.partial(_s5_kernel, nb=nb, tc=tc, reverse=reverse),
        out_shape=(jax.ShapeDtypeStruct((rows, C_W), f32), jax.ShapeDtypeStruct((nb, 2 * S5_STATES), f32)),
        grid=(nc,),
        in_specs=[pl.BlockSpec((rb, C_W), cmap),
                  add_spec,
                  _resident((nb, 2 * S5_STATES)),
                  _resident((1, 2 * S5_STATES)),
                  _resident((2, C_W // 2, S5_STATES)),
                  _resident((2, S5_STATES, C_W // 2)),
                  _resident((C_W, C_W)),
                  _resident((1, C_W))],
        out_specs=(pl.BlockSpec((rb, C_W), cmap),
                   pl.BlockSpec((nb, 2 * S5_STATES), lambda i: (0, 0))),
        scratch_shapes=[pltpu.VMEM((rb, 2 * S5_STATES), f32), pltpu.VMEM((nb, 2 * S5_STATES), f32)],
        compiler_params=_cp("arbitrary"),
        name="s5_rev" if reverse else "s5_fwd",
    )(u, add, h0, a, bm, cm, w_glu, b_glu)


def _sconv_kernel(x_ref, w_ref, b_ref, o_ref):
    x = x_ref[0]
    n = x.shape[0]
    row = lax.broadcasted_iota(jnp.int32, (n, 1), 0)
    prev = jnp.where(row == 0, 0.0, pltpu.roll(x, 1, axis=0))
    nxt = jnp.where(row == n - 1, 0.0, pltpu.roll(x, n - 1, axis=0))
    w = w_ref[...]
    o_ref[0] = prev * w[0:1] + x * w[1:2] + nxt * w[2:3] + b_ref[...]


SCONV_BLOCK_ELEMS = 512 * 1024


def short_conv(x, w, b):
    nb, seq_len, ch = x.shape
    cw = min(ch, max(LANES, SCONV_BLOCK_ELEMS // seq_len // LANES * LANES))
    while ch % cw:
        cw -= LANES
    return pl.pallas_call(
        _sconv_kernel,
        out_shape=jax.ShapeDtypeStruct(x.shape, f32),
        grid=(nb, ch // cw),
        in_specs=[pl.BlockSpec((1, seq_len, cw), lambda b_, j: (b_, 0, j)),
                  pl.BlockSpec((3, cw), lambda b_, j: (0, j)),
                  pl.BlockSpec((1, cw), lambda b_, j: (0, j))],
        out_specs=pl.BlockSpec((1, seq_len, cw), lambda b_, j: (b_, 0, j)),
        compiler_params=_cp("arbitrary", "arbitrary"),
        name="short_conv",
    )(x, w, b)


HY_PAD = LANES
HY_OUT = 4 * D_W


def _filter_kernel(fc_ref, tc_ref, fa_ref, ta_ref, sa_ref, w1_ref, b1_ref, fr1_ref, w2_ref, b2_ref, fr2_ref, w3_ref,
                   dec_ref, oc_ref, oa_ref):
    half = HY_OUT // 2

    def hidden(f_ref):
        z = jnp.sin(fr1_ref[...] * (_bdot(f_ref[...], w1_ref[...]) + b1_ref[...]))
        return jnp.sin(fr2_ref[...] * (_bdot(z, w2_ref[...]) + b2_ref[...]))

    oc_ref[...] = _bdot(hidden(fc_ref), w3_ref[:, :half]) * jnp.exp(-tc_ref[...] * jnp.abs(dec_ref[:, :half]))
    oa_ref[...] = sa_ref[...] * (_bdot(hidden(fa_ref), w3_ref[:, half:]) * jnp.exp(-ta_ref[...] * jnp.abs(dec_ref[:, half:])))


def _pad2(x, rows, cols):
    return jnp.pad(x.astype(f32), ((0, rows - x.shape[0]), (0, cols - x.shape[1])))


def _filter_features(pos, seq_len):
    tn = pos.astype(f32) / seq_len
    bands = jnp.linspace(HY_BAND_MIN, HY_EMB_BANDS - 1, HY_EMB_BANDS, dtype=f32)
    ang = (2.0 * math.pi) * tn[:, None] * bands
    feats = jnp.concatenate([tn[:, None], jnp.cos(ang), jnp.sin(ang)], axis=-1)
    return _pad2(feats, seq_len, HY_PAD), tn[:, None]


def hyena_filters(seq_len, w1, b1, fr1, w2, b2, fr2, w3, decay):
    m = jnp.arange(seq_len)
    feats_c, tn_c = _filter_features(m, seq_len)
    feats_a, tn_a = _filter_features((seq_len - m) % seq_len, seq_len)
    sign_a = jnp.where(m == 0, 0.0, -1.0).astype(f32)[:, None]
    tl = min(seq_len, 512)
    row = lambda v: _pad2(v[None], 1, HY_PAD)
    rows = lambda w: pl.BlockSpec((tl, w), lambda i: (i, 0))
    return pl.pallas_call(
        _filter_kernel,
        out_shape=(jax.ShapeDtypeStruct((seq_len, HY_OUT // 2), f32), jax.ShapeDtypeStruct((seq_len, HY_OUT // 2), f32)),
        grid=(seq_len // tl,),
        in_specs=[rows(HY_PAD), rows(1), rows(HY_PAD), rows(1), rows(1),
                  _resident((HY_PAD, HY_PAD)), _resident((1, HY_PAD)), _resident((1, HY_PAD)),
                  _resident((HY_PAD, HY_PAD)), _resident((1, HY_PAD)), _resident((1, HY_PAD)),
                  _resident((HY_PAD, HY_OUT)), _resident((1, HY_OUT))],
        out_specs=(rows(HY_OUT // 2), rows(HY_OUT // 2)),
        compiler_params=_cp("arbitrary"),
        name="hyena_filters",
    )(feats_c, tn_c, feats_a, tn_a, sign_a, _pad2(w1, HY_PAD, HY_PAD), row(b1), row(fr1), _pad2(w2, HY_PAD, HY_PAD),
      row(b2), row(fr2), _pad2(w3, HY_PAD, HY_OUT), decay.astype(f32).reshape(1, HY_OUT))


HY_N2 = 128
HY_NJ = 16


def _dft_consts(seq_len):
    n = 2 * seq_len
    n1h = seq_len // HY_N2
    k1 = np.arange(n1h)
    th = 2.0 * np.pi * np.outer(k1 + 0.5, np.arange(n1h)) / (2 * n1h)
    f_a = np.concatenate([np.cos(th), -np.sin(th)], axis=0)
    f_c = np.concatenate([np.cos(th).T, -np.sin(th).T], axis=1)
    n2 = np.arange(HY_N2)
    ph = -2.0 * np.pi * (n2[None, None, :] * (k1[:, None, None] + 0.5) / n + np.outer(n2, n2)[None] / HY_N2)
    gr, gi = np.cos(ph), np.sin(ph)
    g = np.concatenate([np.concatenate([gr, -gi], 2), np.concatenate([gi, gr], 2)], 1)
    hr, hi = np.transpose(gr, (0, 2, 1)), -np.transpose(gi, (0, 2, 1))
    h = np.concatenate([np.concatenate([hr, -hi], 2), np.concatenate([hi, hr], 2)], 1)
    return (jnp.asarray(f_a, f32), jnp.asarray(f_c, f32), jnp.asarray(g, f32), jnp.asarray(h, f32))


def _dot_p(a, b, exact):
    if exact:
        return jnp.dot(a, b, preferred_element_type=f32, precision=lax.Precision.HIGHEST)
    return _bdot(a, b)


def _hy_a_kernel(z_ref, f_ref, a_ref, *, exact):
    f = f_ref[...]
    n1h = f.shape[1]
    for j in range(HY_NJ):
        a = _dot_p(f, z_ref[0, :, j, :], exact)
        a_ref[0, 0, :, j, :] = a[:n1h]
        a_ref[0, 1, :, j, :] = a[n1h:]


def hyena_stage_a(z4, col_block, width, f_a, *, exact=False):
    nb, n1h = z4.shape[0], z4.shape[1]
    return pl.pallas_call(
        functools.partial(_hy_a_kernel, exact=exact),
        out_shape=jax.ShapeDtypeStruct((nb, 2, n1h, HY_N2, width), f32),
        grid=(nb, HY_N2 // HY_NJ),
        in_specs=[pl.BlockSpec((1, n1h, HY_NJ, width), lambda b, j: (b, 0, j, col_block)),
                  _resident((2 * n1h, n1h))],
        out_specs=pl.BlockSpec((1, 2, n1h, HY_NJ, width), lambda b, j: (b, 0, 0, j, 0)),
        compiler_params=_cp("arbitrary", "arbitrary"),
        name="hyena_stage_a",
    )(z4, f_a)


def _hy_spec_kernel(a_ref, g_ref, ks_ref, *, scale):
    k1 = pl.program_id(0)
    sgn = (1 - 2 * (k1 % 2)).astype(f32)
    g = g_ref[0]
    kc = _dot_p(g, jnp.concatenate([a_ref[0, 0, 0], a_ref[0, 1, 0]], axis=0), True)
    kg = _dot_p(g, jnp.concatenate([a_ref[1, 0, 0], a_ref[1, 1, 0]], axis=0), True)
    ks_ref[0, 0] = (kc[:HY_N2] + sgn * kg[HY_N2:]) * scale
    ks_ref[1, 0] = (kc[HY_N2:] - sgn * kg[:HY_N2]) * scale


def hyena_filter_spectrum(a_taps, g_mats, *, seq_len):
    n1h = a_taps.shape[2]
    width = a_taps.shape[-1]
    return pl.pallas_call(
        functools.partial(_hy_spec_kernel, scale=1.0 / seq_len),
        out_shape=jax.ShapeDtypeStruct((2, n1h, HY_N2, width), f32),
        grid=(n1h,),
        in_specs=[pl.BlockSpec((2, 2, 1, HY_N2, width), lambda k: (0, 0, k, 0, 0)),
                  pl.BlockSpec((1, 2 * HY_N2, 2 * HY_N2), lambda k: (k, 0, 0))],
        out_specs=pl.BlockSpec((2, 1, HY_N2, width), lambda k: (0, k, 0, 0)),
        compiler_params=_cp("arbitrary"),
        name="hyena_filter_spectrum",
    )(a_taps, g_mats)


HY_BB = 4


def _hy_b_kernel(a_ref, g_ref, h_ref, ks_ref, c_ref):
    g, h = g_ref[0], h_ref[0]
    kr, ki = ks_ref[0, 0], ks_ref[1, 0]
    for b in range(a_ref.shape[0]):
        s = _bdot(g, jnp.concatenate([a_ref[b, 0, 0], a_ref[b, 1, 0]], axis=0))
        sr, si = s[:HY_N2], s[HY_N2:]
        c = _bdot(h, jnp.concatenate([sr * kr - si * ki, sr * ki + si * kr], axis=0))
        c_ref[b, 0, 0] = c[:HY_N2]
        c_ref[b, 1, 0] = c[HY_N2:]


def hyena_stage_b(a, g_mats, h_mats, kspec, order):
    nb, _, n1h, _, width = a.shape
    bb = math.gcd(nb, HY_BB)
    return pl.pallas_call(
        _hy_b_kernel,
        out_shape=jax.ShapeDtypeStruct(a.shape, f32),
        grid=(n1h, nb // bb),
        in_specs=[pl.BlockSpec((bb, 2, 1, HY_N2, width), lambda k, b: (b, 0, k, 0, 0)),
                  pl.BlockSpec((1, 2 * HY_N2, 2 * HY_N2), lambda k, b: (k, 0, 0)),
                  pl.BlockSpec((1, 2 * HY_N2, 2 * HY_N2), lambda k, b: (k, 0, 0)),
                  pl.BlockSpec((2, 1, HY_N2, width), lambda k, b: (0, k, 0, order))],
        out_specs=pl.BlockSpec((bb, 2, 1, HY_N2, width), lambda k, b: (b, 0, k, 0, 0)),
        compiler_params=_cp("arbitrary", "arbitrary"),
        name="hyena_stage_b",
    )(a, g_mats, h_mats, kspec)


def _hy_c_kernel(c_ref, f_ref, x_ref, v_ref, bias_ref, o_ref, y_ref):
    f = f_ref[...]
    for j in range(HY_NJ):
        c = jnp.concatenate([c_ref[0, 0, :, j, :], c_ref[0, 1, :, j, :]], axis=0)
        y_ref[:, j, :] = _bdot(f, c)
    v = v_ref[0]
    o_ref[0] = x_ref[0] * (y_ref[...] + bias_ref[...].reshape(1, 1, -1) * v)


def hyena_stage_c(c, f_c, x4, x_block, v4, v_block, bias):
    nb, _, n1h, _, width = c.shape
    blk = lambda col: pl.BlockSpec((1, n1h, HY_NJ, width), lambda b, j: (b, 0, j, col))
    return pl.pallas_call(
        _hy_c_kernel,
        out_shape=jax.ShapeDtypeStruct((nb, n1h, HY_N2, width), f32),
        grid=(nb, HY_N2 // HY_NJ),
        in_specs=[pl.BlockSpec((1, 2, n1h, HY_NJ, width), lambda b, j: (b, 0, 0, j, 0)),
                  _resident((n1h, 2 * n1h)),
                  blk(x_block), blk(v_block),
                  _resident((1, width))],
        out_specs=blk(0),
        scratch_shapes=[pltpu.VMEM((n1h, HY_NJ, width), f32)],
        compiler_params=_cp("arbitrary", "arbitrary"),
        name="hyena_stage_c",
    )(c, f_c, x4, v4, bias)


def hyena_lat(s, taps, bias):
    nb, seq_len, _ = s.shape
    n1h = seq_len // HY_N2
    f_a, f_c, g_mats, h_mats = _dft_consts(seq_len)
    taps4 = jnp.stack(taps).reshape(2, n1h, HY_N2, 2 * D_W)
    kspec = hyena_filter_spectrum(hyena_stage_a(taps4, 0, 2 * D_W, f_a, exact=True), g_mats, seq_len=seq_len)
    g_bf, h_bf = g_mats.astype(bf16), h_mats.astype(bf16)
    s4 = s.reshape(nb, n1h, HY_N2, 3 * D_W)
    c0 = hyena_stage_b(hyena_stage_a(s4, 0, D_W, f_a), g_bf, h_bf, kspec, 0)
    z1 = hyena_stage_c(c0, f_c, s4, 1, s4, 0, bias[0:1])
    c1 = hyena_stage_b(hyena_stage_a(z1, 0, D_W, f_a), g_bf, h_bf, kspec, 1)
    z2 = hyena_stage_c(c1, f_c, s4, 2, z1, 0, bias[1:2])
    return z2.reshape(nb, seq_len, D_W)


def _ctx_dft_consts(seq_len):
    n = 2 * seq_len
    k = np.arange(seq_len) + 0.5
    th = 2.0 * np.pi * np.outer(k, np.arange(n)) / n
    fwd_full = np.concatenate([np.cos(th), -np.sin(th)], axis=0)
    inv = np.concatenate([np.cos(th[:, :seq_len]).T, -np.sin(th[:, :seq_len]).T], axis=1) * (2.0 / n)
    return jnp.asarray(fwd_full, f32), jnp.asarray(fwd_full[:, :seq_len], f32), jnp.asarray(inv, f32)


def _ctx_spec_kernel(f_ref, t_ref, o_ref):
    o_ref[...] = _dot_p(f_ref[...], t_ref[...], True)


def _hy_ctx_kernel(v_ref, x1_ref, x2_ref, k0_ref, k1_ref, ff_ref, fi_ref, bias_ref, o_ref):
    ff, fi = ff_ref[...], fi_ref[...]
    n = ff.shape[0] // 2
    z = v_ref[0]
    for x_ref, k_ref, o in ((x1_ref, k0_ref, 0), (x2_ref, k1_ref, 1)):
        s = _bdot(ff, z)
        sr, si = s[:n], s[n:]
        kr, ki = k_ref[:n], k_ref[n:]
        y = _bdot(fi, jnp.concatenate([sr * kr - si * ki, sr * ki + si * kr], axis=0))
        z = x_ref[0] * (y + bias_ref[o:o + 1] * z)
    o_ref[0] = z


HY_CTX_CB = 256


def hyena_ctx(s, taps, bias):
    nb, seq_len, _ = s.shape
    n = 2 * seq_len
    fwd_full, fwd, inv = _ctx_dft_consts(seq_len)
    taps = jnp.concatenate(taps, axis=0)
    kspec = pl.pallas_call(
        _ctx_spec_kernel,
        out_shape=jax.ShapeDtypeStruct((n, 2 * D_W), f32),
        name="hyena_ctx_spectrum",
    )(fwd_full, taps)
    ncb = D_W // HY_CTX_CB
    col = lambda off: pl.BlockSpec((1, seq_len, HY_CTX_CB), lambda b, j: (b, 0, off + j))
    return pl.pallas_call(
        _hy_ctx_kernel,
        out_shape=jax.ShapeDtypeStruct((nb, seq_len, D_W), f32),
        grid=(nb, ncb),
        in_specs=[col(0), col(ncb), col(2 * ncb),
                  pl.BlockSpec((n, HY_CTX_CB), lambda b, j: (0, j)),
                  pl.BlockSpec((n, HY_CTX_CB), lambda b, j: (0, ncb + j)),
                  _resident((n, seq_len)), _resident((seq_len, n)),
                  pl.BlockSpec((2, HY_CTX_CB), lambda b, j: (0, j))],
        out_specs=pl.BlockSpec((1, seq_len, HY_CTX_CB), lambda b, j: (b, 0, j)),
        compiler_params=_cp("arbitrary", "arbitrary"),
        name="hyena_ctx",
    )(s, s, s, kspec, kspec, fwd.astype(bf16), inv.astype(bf16), bias)


ROW_TILE = 512
ODD_ROWS = 512


def _even_layer(l, xc, xs, mod_c, mod_s, dims, caches, g_pre, g_post, w_in, w_out, rpb, lam_rows, gs, rope_t):
    (nbc, lc), (nbs, ls) = dims
    e = l // 2
    lam_init = 0.8 - 0.6 * math.exp(-0.3 * l)
    lamv = jnp.zeros((8, LANES), f32).at[:4, :DQK].set(lam_rows)
    w_in, w_out = w_in.astype(bf16), w_out.astype(bf16)
    cak, cav, cbk, cbv = caches
    cos_t, sin_t = rope_t
    pc = even_proj(xc, mod_c, g_pre, w_in, cos_t[:ROW_TILE], sin_t[:ROW_TILE], seq_len=lc, rope=False,
                   out_dtype=f32, tm=ROW_TILE)
    ps = even_proj(xs, mod_s, g_pre, w_in, cos_t, sin_t, seq_len=ls, rope=True, out_dtype=bf16, tm=ROW_TILE)
    oa_c, ob_c = ctx_even_attn(pc, lamv, gs, lam_init=lam_init, n_seq=nbc, seq_len=lc)
    tl, tr = na_bias_tables(rpb)
    oa_s = lat_na_attn(ps, cak, cav, tl, tr, e=e, n_seq=nbs, seq_len=ls)
    ob_s = lat_diff_attn(ps, cbk, cbv, lamv, gs, e=e, lam_init=lam_init, n_seq=nbs, seq_len=ls)
    xc = even_out(xc, oa_c, ob_c, mod_c, g_post, w_out, seq_len=lc, tm=ROW_TILE)
    xs = even_out(xs, oa_s, ob_s, mod_s, g_post, w_out, seq_len=ls, tm=ROW_TILE)

    def unblock(lo):
        return jnp.transpose(pc[lo:lo + 4], (1, 0, 2)).reshape(nbc, lc, D // 2)

    new_cache = (unblock(4).reshape(nbc, lc, HA, DH), unblock(8).reshape(nbc, lc, HA, DH),
                 unblock(16).reshape(nbc, lc, HB, 2 * DQK), unblock(20).reshape(nbc, lc, HB, DVB))
    return xc, xs, new_cache


def _odd_mixer(x, mod, h0, g_pre, g_post, w_in, w_out, s5p, glu, hyp):
    nb, seq_len, _ = x.shape
    tt = ODD_ROWS // nb
    u, hy = odd_proj(x, mod, g_pre, w_in, tt=tt)
    conv_w, conv_b, filt, bias = hyp
    s = short_conv(hy, conv_w, conv_b)
    taps = hyena_filters(seq_len, *filt)
    yd = hyena_lat(s, taps, bias) if seq_len > 2 * HY_N2 else hyena_ctx(s, taps, bias)
    w_glu, b_glu, d_skip = glu
    y = d_skip
    fins = []
    u2 = u.reshape(seq_len * nb, C_W)
    for dr in range(2):
        a, bm, cm = s5p[dr]
        y, fin = s5_direction(u2, y, h0[dr], a, bm, cm, w_glu, b_glu, nb=nb, tc=tt, reverse=(dr == 1))
        fins.append(fin)
    x = odd_out(x, y.reshape(seq_len, nb, C_W), yd, mod, g_post, w_out, tt=tt)
    return x, fins


def kernel(x_prompt, x_sample, cache_a_k, cache_a_v, cache_b_k, cache_b_v, state_c_re, state_c_im, c, c_ctx, w_mod, b_mod, g_mix_pre, g_mix_post, g_ffn_pre, g_ffn_post, w_ffn_gate, w_ffn_up, w_ffn_down, w_in_e, w_out_e, na_rpb, lam_q1, lam_k1, lam_q2, lam_k2, g_subln, w_in_o, w_out_o, ssm_lam_re, ssm_lam_im, ssm_log_dt, ssm_b_re, ssm_b_im, ssm_c_re, ssm_c_im, ssm_d, w_glu, b_glu, hy_conv_w, hy_conv_b, hy_w1, hy_b1, hy_fr1, hy_w2, hy_b2, hy_fr2, hy_w3, hy_decay, hy_bias):
    nbc, lc, _ = x_prompt.shape
    nbs, ls, _ = x_sample.shape
    n_even, past = cache_a_k.shape[1], cache_a_k.shape[2]
    dims = ((nbc, lc), (nbs, ls))
    cond = jnp.concatenate([c_ctx[None], c, jnp.zeros((16 - 1 - nbs, D), f32)], axis=0)
    mod_all = modulation(cond, w_mod, b_mod)
    xc = x_prompt.reshape(nbc * lc, D)
    xs = x_sample.reshape(nbs * ls, D)
    caches = tuple(t.reshape(nbs, n_even, past, D // 2) for t in (cache_a_k, cache_a_v, cache_b_k, cache_b_v))
    rope_t = rope_tables(ls)
    new_caches, new_sre, new_sim = [], [], []
    row = lambda v: v[None]
    for l in range(DEPTH):
        mod_c = mod_all[l, 0:1].reshape(1, 1, 6 * D)
        mod_s = mod_all[l, 1:1 + nbs].reshape(nbs, 1, 6 * D)
        if l % 2 == 0:
            e = l // 2
            lam_rows = jnp.stack([lam_q1[e], lam_k1[e], lam_q2[e], lam_k2[e]])
            xc, xs, nc = _even_layer(l, xc, xs, mod_c, mod_s, dims, caches, row(g_mix_pre[l]), row(g_mix_post[l]),
                                     w_in_e[e], w_out_e[e], na_rpb[e], lam_rows, row(g_subln[e]), rope_t)
            new_caches.append(nc)
        else:
            o = l // 2
            s5p = [s5_prepare(ssm_lam_re[o, dr], ssm_lam_im[o, dr], ssm_log_dt[o, dr], ssm_b_re[o, dr], ssm_b_im[o, dr],
                              ssm_c_re[o, dr], ssm_c_im[o, dr]) for dr in range(2)]
            glu = (w_glu[o].astype(bf16), row(b_glu[o]), row(ssm_d[o]))
            filt = (hy_w1[o], hy_b1[o], hy_fr1[o], hy_w2[o], hy_b2[o], hy_fr2[o], hy_w3[o], hy_decay[o])
            hyp = (hy_conv_w[o], row(hy_conv_b[o]), filt, hy_bias[o])
            w_in, w_out = w_in_o[o].astype(bf16), w_out_o[o].astype(bf16)
            zero_state = jnp.zeros((nbc, 2 * S5_STATES), f32)
            xc3, fins = _odd_mixer(xc.reshape(nbc, lc, D), jnp.broadcast_to(mod_c, (nbc, 1, 6 * D)), (zero_state, zero_state),
                                   row(g_mix_pre[l]), row(g_mix_post[l]), w_in, w_out, s5p, glu, hyp)
            h0 = [jnp.concatenate([state_c_re[:, o, dr].reshape(nbs, S5_STATES), state_c_im[:, o, dr].reshape(nbs, S5_STATES)],
                                  axis=1) for dr in range(2)]
            xs3, _ = _odd_mixer(xs.reshape(nbs, ls, D), mod_s, h0, row(g_mix_pre[l]), row(g_mix_post[l]), w_in, w_out,
                                s5p, glu, hyp)
            xc, xs = xc3.reshape(nbc * lc, D), xs3.reshape(nbs * ls, D)
            new_sre.append(jnp.stack([f[:, :S5_STATES].reshape(nbc, C_G, C_P) for f in fins], axis=1))
            new_sim.append(jnp.stack([f[:, S5_STATES:].reshape(nbc, C_G, C_P) for f in fins], axis=1))
        wg, wu, wd = w_ffn_gate[l].astype(bf16), w_ffn_up[l].astype(bf16), w_ffn_down[l].astype(bf16)
        xc = ffn(xc, mod_c, row(g_ffn_pre[l]), row(g_ffn_post[l]), wg, wu, wd, seq_len=lc, tm=ROW_TILE)
        xs = ffn(xs, mod_s, row(g_ffn_pre[l]), row(g_ffn_post[l]), wg, wu, wd, seq_len=ls, tm=ROW_TILE)
    new_a_k, new_a_v, new_b_k, new_b_v = (jnp.stack([nc[i] for nc in new_caches], axis=1) for i in range(4))
    return (xc.reshape(nbc, lc, D), xs.reshape(nbs, ls, D), new_a_k, new_a_v, new_b_k, new_b_v,
            jnp.stack(new_sre, axis=1), jnp.stack(new_sim, axis=1))
```
